```python
import jax, jax.numpy as jnp
from jax import lax
import numpy as np

D_MODEL = 1024
BATCH = 2
SEQ = 16384
DEPTH = 2
DEC_BATCH = 16
DEC_SEQ = 32
PAST_LEN = 2048

CHUNK = 64
WA = 256
CONV_W = 31
HB = 4
DKB = 128
DVB = 128
WB = HB * DVB
HC = 4
DHC = 64
WC = HC * DHC
BAND_CHUNKS = 8
BAND_PAST = BAND_CHUNKS * CHUNK
BAND_LEN = (BAND_CHUNKS + 1) * CHUNK
REL_CLIP = 128
D_FF = 2816
FFN_CONV_W = 3
EPS = 1e-6
N_A = 2 * WA
N_BQ = HB * DKB
N_BF = HB * DKB
N_BI = HB * DVB
N_BG = WB
N_C = 3 * WC
N_GATE = 3 * D_MODEL
N_IN = N_A + N_BQ + N_BF + N_BI + N_BG + N_C + N_GATE
SPLIT_IDX = (N_A, N_A + N_BQ, N_A + N_BQ + N_BF, N_A + N_BQ + N_BF + N_BI,
             N_A + N_BQ + N_BF + N_BI + N_BG, N_A + N_BQ + N_BF + N_BI + N_BG + N_C)

kernel_name = 'hybrid_stream_conv_hgrn2_bandattn_step'


def rmsnorm(x, g):
    xf = x.astype(jnp.float32)
    y = xf * lax.rsqrt(jnp.mean(xf * xf, axis=-1, keepdims=True) + EPS)
    return (y * g).astype(x.dtype)


def layernorm(x, g, b):
    xf = x.astype(jnp.float32)
    mu = jnp.mean(xf, axis=-1, keepdims=True)
    xc = xf - mu
    var = jnp.mean(xc * xc, axis=-1, keepdims=True)
    return (xc * lax.rsqrt(var + EPS) * g + b).astype(x.dtype)


def dwconv_valid(x_ext, w):
    c = x_ext.shape[-1]
    return lax.conv_general_dilated(x_ext, w.astype(x_ext.dtype)[:, None, :], window_strides=(1,),
                                    padding='VALID', dimension_numbers=('NWC', 'WIO', 'NWC'),
                                    feature_group_count=c)


def conv_module(a_glu, buf, dw_w, dw_b, ln_g, ln_b, w_out):
    u = a_glu[..., :WA] * jax.nn.sigmoid(a_glu[..., WA:])
    ext = jnp.concatenate([buf, u], axis=1)
    h = dwconv_valid(ext, dw_w) + dw_b
    h = jax.nn.silu(layernorm(h, ln_g, ln_b))
    return h @ w_out, ext[:, -(CONV_W - 1):]


def hgrn_chunk(S, q, logf, k, v):
    L = q.shape[2]
    b = jnp.cumsum(logf, axis=2)
    o_inter = jnp.einsum('nhtk,nhkv->nhtv', q * jnp.exp(b), S)
    causal = jnp.tril(jnp.ones((L, L), dtype=bool))[:, :, None]
    diff = b[:, :, :, None, :] - b[:, :, None, :, :]
    decay = jnp.where(causal, jnp.exp(jnp.where(causal, diff, 0.0)), 0.0)
    A = jnp.einsum('nhtk,nhtsk,nhsk->nhts', q, decay, k)
    o = o_inter + jnp.einsum('nhts,nhsv->nhtv', A, v)
    b_last = b[:, :, -1:, :]
    S_new = jnp.exp(b_last[:, :, 0, :, None]) * S + jnp.einsum('nhsk,nhsv->nhkv', k * jnp.exp(b_last - b), v)
    return S_new, o


def hgrn_mixer(S0, q_in, f_in, i_in, g_in, lb, norm_g, w_out, blk):
    N, T, _ = q_in.shape
    nb = T // blk

    def heads(z, d):
        return z.reshape(N, nb, blk, HB, d).transpose(1, 0, 3, 2, 4).astype(jnp.float32)

    zf = f_in.astype(jnp.float32)
    f = lb + (1.0 - lb) * jax.nn.sigmoid(zf)
    log_f = jnp.log(f)
    k = (1.0 - lb) * jax.nn.sigmoid(-zf)
    q = jax.nn.silu(q_in)
    xs = (heads(q, DKB), heads(log_f, DKB), heads(k, DKB), heads(i_in, DVB))
    S_fin, o = lax.scan(lambda S, c: hgrn_chunk(S, *c), S0.astype(jnp.float32), xs)
    o = o.transpose(1, 0, 3, 2, 4).reshape(N, T, HB, DVB)
    o = o * lax.rsqrt(jnp.mean(o * o, axis=-1, keepdims=True) + EPS)
    o = (o.reshape(N, T, WB) * norm_g).astype(q_in.dtype) * jax.nn.silu(g_in)
    return o @ w_out, S_fin


def rel_bias_table(tab, rel):
    return tab[:, jnp.clip(rel, -REL_CLIP, REL_CLIP) + REL_CLIP]


def attend(q, k, v, bias, valid):
    s = jnp.einsum('...qhd,...khd->...hqk', q, k).astype(jnp.float32) * (DHC ** -0.5) + bias
    s = jnp.where(valid[..., None, None, :], s, -1e30)
    p = jax.nn.softmax(s, axis=-1).astype(v.dtype)
    return jnp.einsum('...hqk,...khd->...qhd', p, v)


def conv_ffn(h, buf, w_up, dw_w, w_down):
    u = h @ w_up
    ext = jnp.concatenate([buf, u], axis=1)
    u = dwconv_valid(ext, dw_w)
    a, b = jnp.split(u, 2, axis=-1)
    return (jax.nn.silu(a) * b) @ w_down, ext[:, -(FFN_CONV_W - 1):]


def setup_inputs(seed: int = 0) -> dict:
    key = jax.random.key(seed)
    ks = jax.random.split(key, 26)
    nrm = jax.random.normal
    f32 = jnp.float32
    c_cache = min(BAND_PAST, PAST_LEN)
    return {
        'x_prompt': nrm(ks[0], (BATCH, SEQ, D_MODEL), f32),
        'x_sample': nrm(ks[1], (DEC_BATCH, DEC_SEQ, D_MODEL), f32),
        'state_conv': 0.5 * nrm(ks[2], (DEPTH, DEC_BATCH, CONV_W - 1, WA), f32),
        'state_hgrn': 0.5 * nrm(ks[3], (DEPTH, DEC_BATCH, HB, DKB, DVB), f32),
        'cache_attn_k': nrm(ks[4], (DEPTH, DEC_BATCH, c_cache, HC, DHC), f32),
        'cache_attn_v': nrm(ks[5], (DEPTH, DEC_BATCH, c_cache, HC, DHC), f32),
        'state_ffn': nrm(ks[6], (DEPTH, DEC_BATCH, FFN_CONV_W - 1, 2 * D_FF), f32),
        'w_in': nrm(ks[7], (DEPTH, D_MODEL, N_IN), f32) * D_MODEL ** -0.5,
        'conv_dw_w': nrm(ks[8], (DEPTH, CONV_W, WA), f32) * CONV_W ** -0.5,
        'conv_dw_b': 0.01 * nrm(ks[9], (DEPTH, WA), f32),
        'conv_ln_g': 1.0 + 0.02 * nrm(ks[10], (DEPTH, WA), f32),
        'conv_ln_b': 0.02 * nrm(ks[11], (DEPTH, WA), f32),
        'w_conv_out': nrm(ks[12], (DEPTH, WA, D_MODEL), f32) * WA ** -0.5,
        'hgrn_lb_logits': 0.1 * nrm(ks[13], (DEPTH, HB * DKB), f32),
        'hgrn_norm_g': 1.0 + 0.02 * nrm(ks[14], (DEPTH, WB), f32),
        'w_hgrn_out': nrm(ks[15], (DEPTH, WB, D_MODEL), f32) * WB ** -0.5,
        'attn_rel_bias': 0.5 * nrm(ks[16], (DEPTH, HC, 2 * REL_CLIP + 1), f32),
        'w_attn_out': nrm(ks[17], (DEPTH, WC, D_MODEL), f32) * WC ** -0.5,
        'w_mix_out': nrm(ks[18], (DEPTH, D_MODEL, D_MODEL), f32) * D_MODEL ** -0.5,
        'g_mix': 1.0 + 0.02 * nrm(ks[19], (DEPTH, D_MODEL), f32),
        'w_ffn_up': nrm(ks[20], (DEPTH, D_MODEL, 2 * D_FF), f32) * D_MODEL ** -0.5,
        'ffn_dw_w': nrm(ks[21], (DEPTH, FFN_CONV_W, 2 * D_FF), f32) * FFN_CONV_W ** -0.5,
        'w_ffn_down': nrm(ks[22], (DEPTH, D_FF, D_MODEL), f32) * D_FF ** -0.5,
        'g_ffn': 1.0 + 0.02 * nrm(ks[23], (DEPTH, D_MODEL), f32),
        'g_final': 1.0 + 0.02 * nrm(ks[24], (D_MODEL,), f32),
    }


def reference(x_prompt, x_sample, state_conv, state_hgrn, cache_attn_k, cache_attn_v, state_ffn,
              w_in, conv_dw_w, conv_dw_b, conv_ln_g, conv_ln_b, w_conv_out,
              hgrn_lb_logits, hgrn_norm_g, w_hgrn_out, attn_rel_bias, w_attn_out,
              w_mix_out, g_mix, w_ffn_up, ffn_dw_w, w_ffn_down, g_ffn, g_final):
    p_lb = jax.nn.softmax(hgrn_lb_logits.astype(jnp.float32), axis=0)
    lbs = jnp.cumsum(p_lb, axis=0) - p_lb[0]

    def attn_prompt(q, k, v, rel_tab, l):
        N, T = q.shape[0], q.shape[1]
        nC = T // CHUNK

        def band(t):
            tc = t.reshape(N, nC, CHUNK, HC, DHC)
            tp = jnp.pad(tc, ((0, 0), (BAND_CHUNKS, 0), (0, 0), (0, 0), (0, 0)))
            return jnp.concatenate([tp[:, j:j + nC] for j in range(BAND_CHUNKS + 1)], axis=2)

        qpos = jnp.arange(CHUNK) + BAND_PAST
        kpos = jnp.arange(BAND_LEN)
        bias = rel_bias_table(rel_tab, qpos[:, None] - kpos[None, :])
        valid = (jnp.arange(nC)[:, None] - BAND_CHUNKS) * CHUNK + kpos[None, :] >= 0
        o = attend(q.reshape(N, nC, CHUNK, HC, DHC), band(k), band(v), bias, valid)
        return o.reshape(N, T, HC, DHC), k[:, -BAND_PAST:], v[:, -BAND_PAST:]

    def attn_sample(q, k, v, rel_tab, l):
        T = q.shape[1]
        c_cache = cache_attn_k.shape[2]
        kc = jnp.concatenate([cache_attn_k[l], k], axis=1)
        vc = jnp.concatenate([cache_attn_v[l], v], axis=1)
        qpos = PAST_LEN + jnp.arange(T)
        kpos = jnp.concatenate([PAST_LEN - c_cache + jnp.arange(c_cache), PAST_LEN + jnp.arange(T)])
        bias = rel_bias_table(rel_tab, qpos[:, None] - kpos[None, :])
        valid = kpos >= (PAST_LEN // CHUNK - BAND_CHUNKS) * CHUNK
        return attend(q, kc, vc, bias, valid), k, v

    def run_layer(x, l, conv_buf, S0, ffn_buf, attn_fn, blk):
        N, T, _ = x.shape
        xn = rmsnorm(x, g_mix[l])
        z = xn @ w_in[l]
        a_glu, bq, bf, bi, bg, c_qkv, gates = jnp.split(z, SPLIT_IDX, axis=-1)
        yA, conv_new = conv_module(a_glu, conv_buf, conv_dw_w[l], conv_dw_b[l],
                                   conv_ln_g[l], conv_ln_b[l], w_conv_out[l])
        yB, S_new = hgrn_mixer(S0, bq, bf, bi, bg, lbs[l], hgrn_norm_g[l], w_hgrn_out[l], blk)
        cq, ck, cv = jnp.split(c_qkv, 3, axis=-1)
        oC, k_new, v_new = attn_fn(cq.reshape(N, T, HC, DHC), ck.reshape(N, T, HC, DHC),
                                   cv.reshape(N, T, HC, DHC), attn_rel_bias[l], l)
        yC = oC.reshape(N, T, WC) @ w_attn_out[l]
        g = jax.nn.sigmoid(gates)
        m = g[..., :D_MODEL] * yA + g[..., D_MODEL:2 * D_MODEL] * yB + g[..., 2 * D_MODEL:] * yC
        h = x + m @ w_mix_out[l]
        f, ffn_new = conv_ffn(rmsnorm(h, g_ffn[l]), ffn_buf, w_ffn_up[l], ffn_dw_w[l], w_ffn_down[l])
        return h + f, conv_new, S_new, k_new, v_new, ffn_new

    nP = x_prompt.shape[0]
    zero_conv = jnp.zeros((nP, CONV_W - 1, WA), x_prompt.dtype)
    zero_S = jnp.zeros((nP, HB, DKB, DVB), jnp.float32)
    zero_ffn = jnp.zeros((nP, FFN_CONV_W - 1, 2 * D_FF), x_prompt.dtype)

    xp, xs = x_prompt, x_sample
    cp, cs, sp, ss, kp, vp, ksn, vsn, fp, fs = [], [], [], [], [], [], [], [], [], []
    for l in range(DEPTH):
        xp, c1, s1, k1, v1, f1 = run_layer(xp, l, zero_conv, zero_S, zero_ffn, attn_prompt, CHUNK)
        xs, c2, s2, k2, v2, f2 = run_layer(xs, l, state_conv[l], state_hgrn[l], state_ffn[l],
                                           attn_sample, x_sample.shape[1])
        cp.append(c1); sp.append(s1); kp.append(k1); vp.append(v1); fp.append(f1)
        cs.append(c2); ss.append(s2); ksn.append(k2); vsn.append(v2); fs.append(f2)

    y_prompt = rmsnorm(xp, g_final)
    y_sample = rmsnorm(xs, g_final)
    new_conv_p = jnp.stack(cp)
    new_conv_s = jnp.stack(cs)
    new_hgrn_p = jnp.stack(sp)
    new_hgrn_s = jnp.stack(ss)
    new_k_p = jnp.stack(kp)
    new_v_p = jnp.stack(vp)
    new_k_s = jnp.stack(ksn)
    new_v_s = jnp.stack(vsn)
    new_ffn_p = jnp.stack(fp)
    new_ffn_s = jnp.stack(fs)
    return (y_prompt, y_sample, new_conv_p, new_conv_s, new_hgrn_p, new_hgrn_s,
            new_k_p, new_v_p, new_k_s, new_v_s, new_ffn_p, new_ffn_s)
```

```python
import functools

import jax
import jax.numpy as jnp
from jax import lax
from jax.experimental import pallas as pl
from jax.experimental.pallas import tpu as pltpu

F32 = jnp.float32
BF16 = jnp.bfloat16

D_MODEL = 1024
CHUNK = 64
WA = 256
CONV_W = 31
HB = 4
DKB = 128
DVB = 128
WB = HB * DVB
HC = 4
DHC = 64
WC = HC * DHC
BAND_CHUNKS = 8
BAND_PAST = BAND_CHUNKS * CHUNK
REL_CLIP = 128
D_FF = 2816
FFN_CONV_W = 3
EPS = 1e-6
PAST_LEN = 2048

C_A = 0
C_BQ = 2 * WA
C_BF = C_BQ + HB * DKB
C_BI = C_BF + HB * DKB
C_BG = C_BI + WB
C_C = C_BG + WB
C_GATE = C_C + 3 * WC
N_IN = C_GATE + 3 * D_MODEL

TILE = 512
SUB = 16
KEYS = 640
CONV_HIST = 32
FF_BLK = 256
N_FF_BLK = D_FF // FF_BLK
NEG = -1e30
SUBLANES = 8
V7X_VMEM_BYTES = 64 * 1024 * 1024
VMEM_LIMIT = V7X_VMEM_BYTES - 8 * 1024 * 1024


def _dot(a, b):
    return jnp.dot(a, b, preferred_element_type=F32)


def _dot_nt(a, b):
    return lax.dot_general(a, b, (((1,), (1,)), ((), ())), preferred_element_type=F32)


def _sig(x):
    return jax.nn.sigmoid(x)


def _silu(x):
    return x * jax.nn.sigmoid(x)


def _rms(x, g):
    return x * lax.rsqrt(jnp.mean(x * x, axis=-1, keepdims=True) + EPS) * g


def _lower_bound(lbl_ref, layer):
    lg = lbl_ref[...]
    e = jnp.exp(lg - jnp.max(lg, axis=0, keepdims=True))
    p = e / jnp.sum(e, axis=0, keepdims=True)
    return jnp.sum(p[0:layer + 1], axis=0, keepdims=True) - p[0:1]


def _project_in(x, win_ref, gmix_ref, lb, xn_s, u_s, q_s, k_s, lf_s, v_s, g_s, aq_s):
    xn = _rms(x, gmix_ref[...]).astype(BF16)
    xn_s[...] = xn
    a = _dot(xn, win_ref[:, C_A:C_A + 2 * WA])
    u_s[...] = a[:, :WA] * _sig(a[:, WA:])
    q_s[...] = _silu(_dot(xn, win_ref[:, C_BQ:C_BF]))
    zf = _dot(xn, win_ref[:, C_BF:C_BI])
    lf_s[...] = jnp.log(lb + (1.0 - lb) * _sig(zf))
    k_s[...] = (1.0 - lb) * _sig(-zf)
    v_s[...] = _dot(xn, win_ref[:, C_BI:C_BG])
    g_s[...] = _silu(_dot(xn, win_ref[:, C_BG:C_C]))
    zc = _dot(xn, win_ref[:, C_C:C_GATE])
    aq_s[...] = zc[:, :WC] * (DHC ** -0.5)
    return zc[:, WC:2 * WC], zc[:, 2 * WC:]


def _conv_chunk(ext_s, dww_ref, dwb_ref, lng_ref, lnb_ref, L):
    lo = CONV_HIST - (CONV_W - 1)
    acc = jnp.zeros((L, WA), F32) + dwb_ref[...]
    for j in range(CONV_W):
        acc = acc + ext_s[lo + j:lo + j + L, :] * dww_ref[j:j + 1, :]
    mu = jnp.mean(acc, axis=-1, keepdims=True)
    xc = acc - mu
    var = jnp.mean(xc * xc, axis=-1, keepdims=True)
    y = xc * lax.rsqrt(var + EPS) * lng_ref[...] + lnb_ref[...]
    return _silu(y)


def _cumsum_rows(x, L):
    r = lax.broadcasted_iota(jnp.int32, (L, L), 0)
    c = lax.broadcasted_iota(jnp.int32, (L, L), 1)
    tri = (r >= c).astype(BF16)
    hi = x.astype(BF16)
    r1 = x - hi.astype(F32)
    mid = r1.astype(BF16)
    lo = (r1 - mid.astype(F32)).astype(BF16)
    return _dot(tri, hi) + _dot(tri, mid) + _dot(tri, lo)


def _hgrn_chunk(q_s, k_s, lf_s, v_s, g_s, ng_ref, r0, L, S_s, c_s, oB_s):
    nb = L // SUB
    rows = pl.ds(r0, L)
    b_s, kc_s, vc_s = c_s.at[0], c_s.at[1], c_s.at[2]
    b_s[...] = _cumsum_rows(lf_s[rows, :], L)
    kc_s[...] = k_s[rows, :]
    vc_s[...] = v_s[rows, :]
    trow = lax.broadcasted_iota(jnp.int32, (SUB, DKB), 0)
    lrow = lax.broadcasted_iota(jnp.int32, (L, DKB), 0)
    for h in range(HB):
        ls = slice(h * DKB, (h + 1) * DKB)
        b = b_s[:, ls]
        q = q_s[rows, ls]
        k = kc_s[:, ls]
        v = vc_s[:, ls]
        st = S_s[h]
        o = _dot_nt((q * jnp.exp(b)).astype(BF16), st.astype(BF16))
        a_rows = [jnp.zeros((SUB, L), F32)]
        for i in range(1, nb):
            ref = b_s[i * SUB - 1:i * SUB, ls]
            qt = q[i * SUB:(i + 1) * SUB] * jnp.exp(b[i * SUB:(i + 1) * SUB] - ref)
            kt = k * jnp.exp(jnp.where(lrow < i * SUB, ref - b, NEG))
            a_rows.append(_dot_nt(qt.astype(BF16), kt.astype(BF16)))
        a = jnp.concatenate(a_rows, axis=0)
        o = o + _dot(a.astype(BF16), v.astype(BF16))
        od = []
        for i in range(nb):
            qb = q[i * SUB:(i + 1) * SUB]
            bb = b[i * SUB:(i + 1) * SUB]
            acc = jnp.zeros((SUB, DVB), F32)
            for s in range(SUB):
                g = i * SUB + s
                bg = b_s[g:g + 1, ls]
                kg = kc_s[g:g + 1, ls]
                vg = vc_s[g:g + 1, ls]
                e = jnp.exp(jnp.where(trow >= s, bb - bg, NEG))
                col = jnp.sum(qb * kg * e, axis=1, keepdims=True)
                acc = acc + col * vg
            od.append(acc)
        o = o + jnp.concatenate(od, axis=0)
        bl = b_s[L - 1:L, ls]
        kt = (k * jnp.exp(bl - b)).astype(BF16)
        S_s[h] = st * jnp.exp(bl) + _dot(v.T.astype(BF16), kt)
        on = o * lax.rsqrt(jnp.mean(o * o, axis=-1, keepdims=True) + EPS)
        oB_s[rows, ls] = (on * ng_ref[:, ls] * g_s[rows, ls]).astype(BF16)


def _attn_chunk(q, kband, vband, bias_ref, valid):
    L = q.shape[0]
    lane = lax.broadcasted_iota(jnp.int32, (1, 2 * DHC), 1)
    outs = []
    for p in range(HC // 2):
        ls = slice(p * 2 * DHC, (p + 1) * 2 * DHC)
        qp = q[:, ls]
        kp = kband[:, ls]
        vp = vband[:, ls]
        acc = jnp.zeros((L, 2 * DHC), F32)
        for hh in range(2):
            h = 2 * p + hh
            mine = (lane >= hh * DHC) & (lane < (hh + 1) * DHC)
            s = _dot_nt(jnp.where(mine, qp, 0.0).astype(BF16), kp) + bias_ref[h]
            if valid is not None:
                s = jnp.where(valid, s, NEG)
            e = jnp.exp(s - jnp.max(s, axis=-1, keepdims=True))
            den = jnp.sum(e, axis=-1, keepdims=True)
            pv = _dot(e.astype(BF16), vp) / den
            acc = acc + jnp.where(mine, pv, 0.0)
        outs.append(acc)
    return jnp.concatenate(outs, axis=1)


def _merge_out(x, xn_s, hA_s, oB_s, oC_s, m_s, win_ref, wc_ref, wh_ref, wa_ref, wm_ref):
    half = D_MODEL // 2
    for cb in range(2):
        cs = slice(cb * half, (cb + 1) * half)
        m = None
        for k, (src, w_ref) in enumerate(((hA_s, wc_ref), (oB_s, wh_ref), (oC_s, wa_ref))):
            g0 = C_GATE + k * D_MODEL + cb * half
            gate = _sig(_dot(xn_s[...], win_ref[:, g0:g0 + half]))
            term = gate * _dot(src[...], w_ref[:, cs])
            m = term if m is None else m + term
        m_s[:, cs] = m.astype(BF16)
    return x + _dot(m_s[...], wm_ref[...])


def _mixer_prompt_kernel(layer, n_tiles,
                         x_ref, win_ref, gmix_ref, dww_ref, dwb_ref, lng_ref, lnb_ref, wc_ref,
                         lbl_ref, ng_ref, wh_ref, bias_ref, wa_ref, wm_ref,
                         h_ref, nconv_ref, ns_ref, nk_ref, nv_ref,
                         xn_s, u_s, q_s, k_s, lf_s, v_s, g_s, aq_s, kb_s, vb_s, ext_s, S_s, b_s,
                         hA_s, oB_s, oC_s, m_s):
    t = pl.program_id(1)

    @pl.when(t == 0)
    def _():
        ext_s[0:CONV_HIST, :] = jnp.zeros((CONV_HIST, WA), F32)
        S_s[...] = jnp.zeros_like(S_s)
        kb_s[...] = jnp.zeros_like(kb_s)
        vb_s[...] = jnp.zeros_like(vb_s)

    @pl.when(t > 0)
    def _():
        kb_s[0:TILE, :] = kb_s[TILE:2 * TILE, :]
        vb_s[0:TILE, :] = vb_s[TILE:2 * TILE, :]

    x = x_ref[0]
    lb = _lower_bound(lbl_ref, layer)
    ck, cv = _project_in(x, win_ref, gmix_ref, lb, xn_s, u_s, q_s, k_s, lf_s, v_s, g_s, aq_s)
    kb_s[TILE:2 * TILE, :] = ck.astype(BF16)
    vb_s[TILE:2 * TILE, :] = cv.astype(BF16)

    @pl.when(t == n_tiles - 1)
    def _():
        nk_ref[0] = ck
        nv_ref[0] = cv

    col = lax.broadcasted_iota(jnp.int32, (1, KEYS), 1)

    def chunk(c, carry):
        r0 = pl.multiple_of(c * CHUNK, CHUNK)
        rows = pl.ds(r0, CHUNK)
        ext_s[CONV_HIST:CONV_HIST + CHUNK, :] = u_s[rows, :]
        hA_s[rows, :] = _conv_chunk(ext_s, dww_ref, dwb_ref, lng_ref, lnb_ref, CHUNK).astype(BF16)
        ext_s[0:CONV_HIST, :] = ext_s[CHUNK:CHUNK + CONV_HIST, :]
        _hgrn_chunk(q_s, k_s, lf_s, v_s, g_s, ng_ref, r0, CHUNK, S_s, b_s, oB_s)
        valid = col >= BAND_PAST - r0 - t * TILE
        keys = pl.ds(r0, KEYS)
        oC_s[rows, :] = _attn_chunk(aq_s[rows, :], kb_s[keys, :], vb_s[keys, :], bias_ref, valid).astype(BF16)
        return carry

    lax.fori_loop(0, TILE // CHUNK, chunk, 0)

    h_ref[0] = _merge_out(x, xn_s, hA_s, oB_s, oC_s, m_s, win_ref, wc_ref, wh_ref, wa_ref, wm_ref)

    @pl.when(t == n_tiles - 1)
    def _():
        nconv_ref[0] = ext_s[CONV_HIST - (CONV_W - 1):CONV_HIST, :]
        for h in range(HB):
            ns_ref[0, h] = S_s[h].T


def _mixer_sample_kernel(layer, n_seq, L,
                         x_ref, conv_ref, s0_ref, cak_ref, cav_ref,
                         win_ref, gmix_ref, dww_ref, dwb_ref, lng_ref, lnb_ref, wc_ref,
                         lbl_ref, ng_ref, wh_ref, bias_ref, wa_ref, wm_ref,
                         h_ref, nconv_ref, ns_ref, nk_ref, nv_ref,
                         xn_s, u_s, q_s, k_s, lf_s, v_s, g_s, aq_s, kb_s, vb_s, ext_s, S_s, b_s,
                         hA_s, oB_s, oC_s, m_s, kn_s, vn_s):
    n = pl.program_id(0)
    n_cache = cak_ref.shape[1]

    @pl.when(n == 0)
    def _():
        lb = _lower_bound(lbl_ref, layer)
        ck, cv = _project_in(x_ref[...], win_ref, gmix_ref, lb, xn_s, u_s, q_s, k_s, lf_s, v_s, g_s, aq_s)
        nk_ref[...] = ck
        nv_ref[...] = cv
        kn_s[...] = ck.astype(BF16)
        vn_s[...] = cv.astype(BF16)
        kb_s[...] = jnp.zeros_like(kb_s)
        vb_s[...] = jnp.zeros_like(vb_s)

    r0 = pl.multiple_of(n * L, L)
    rows = pl.ds(r0, L)
    ext_s[0:CONV_HIST, :] = conv_ref[0]
    ext_s[CONV_HIST:CONV_HIST + L, :] = u_s[rows, :]
    hA_s[rows, :] = _conv_chunk(ext_s, dww_ref, dwb_ref, lng_ref, lnb_ref, L).astype(BF16)
    nconv_ref[0] = ext_s[L + CONV_HIST - (CONV_W - 1):L + CONV_HIST, :]
    for h in range(HB):
        S_s[h] = s0_ref[0, h].T
    _hgrn_chunk(q_s, k_s, lf_s, v_s, g_s, ng_ref, r0, L, S_s, b_s, oB_s)
    for h in range(HB):
        ns_ref[0, h] = S_s[h].T
    kb_s[0:n_cache, :] = cak_ref[0].astype(BF16)
    vb_s[0:n_cache, :] = cav_ref[0].astype(BF16)
    kb_s[n_cache:n_cache + L, :] = kn_s[rows, :]
    vb_s[n_cache:n_cache + L, :] = vn_s[rows, :]
    oC_s[rows, :] = _attn_chunk(aq_s[rows, :], kb_s[...], vb_s[...], bias_ref, None).astype(BF16)

    @pl.when(n == n_seq - 1)
    def _():
        h_ref[...] = _merge_out(x_ref[...], xn_s, hA_s, oB_s, oC_s, m_s, win_ref, wc_ref, wh_ref, wa_ref, wm_ref)


def _ffn_kernel(final, n_tiles, n_seq,
                h_ref, st_ref, gffn_ref, wua_ref, wub_ref, dwa_ref, dwb_ref, wd_ref, gfin_ref,
                out_ref, nffn_ref,
                hn_s, acc_s, tail_s, ext_s):
    t = pl.program_id(1)
    rows_t = h_ref.shape[1]
    keep = 2 * n_seq
    hist = tail_s.shape[1]

    @pl.when(t == 0)
    def _():
        tail_s[...] = jnp.zeros_like(tail_s)
        for jj in range(2 * N_FF_BLK):
            tail_s[jj, hist - keep:hist, :] = st_ref[0, jj]

    h = h_ref[0]
    hn_s[...] = _rms(h, gffn_ref[...]).astype(BF16)
    acc_s[...] = jnp.zeros_like(acc_s)

    def conv3(u, slot, jj, w):
        ext_s[slot, 0:hist, :] = tail_s[jj]
        ext_s[slot, hist:hist + rows_t, :] = u
        tail_s[jj] = u[rows_t - hist:rows_t]
        u1 = ext_s[slot, hist - n_seq:hist - n_seq + rows_t, :]
        u2 = ext_s[slot, hist - keep:hist - keep + rows_t, :]
        return u * w[2:3] + u1 * w[1:2] + u2 * w[0:1]

    def block(j, carry):
        hn = hn_s[...]
        ca = conv3(_dot(hn, wua_ref[j]), 0, j, dwa_ref[j])
        cb = conv3(_dot(hn, wub_ref[j]), 1, N_FF_BLK + j, dwb_ref[j])
        acc_s[...] += _dot((_silu(ca) * cb).astype(BF16), wd_ref[j])
        return carry

    lax.fori_loop(0, N_FF_BLK, block, 0)
    y = h + acc_s[...]
    if final:
        y = _rms(y, gfin_ref[...])
    out_ref[0] = y

    @pl.when(t == n_tiles - 1)
    def _():
        for jj in range(2 * N_FF_BLK):
            nffn_ref[0, jj] = tail_s[jj, hist - keep:hist, :]


def _const_spec(shape):
    zeros = (0,) * len(shape)
    return pl.BlockSpec(shape, lambda *_: zeros, pipeline_mode=pl.Buffered(1))


def _bias_table(tab, n_q, n_k):
    span = n_q + n_k - 1
    n_flat = BAND_PAST + n_q - 1 - REL_CLIP
    lo = BAND_PAST + n_q - 1 - (span - 1) + REL_CLIP
    assert lo >= 0 and n_flat >= 0
    g = jnp.concatenate([jnp.broadcast_to(tab[:, 2 * REL_CLIP:], (HC, n_flat)),
                         tab[:, lo:2 * REL_CLIP + 1][:, ::-1]], axis=1)
    rows = [g[:, n_q - 1 - i:n_q - 1 - i + n_k] for i in range(n_q)]
    bias = jnp.stack(rows, axis=1)
    return jnp.pad(bias, ((0, 0), (0, 0), (0, KEYS - n_k)), constant_values=NEG)


def _mixer_scratch(rows, L, kb_rows):
    return [
        pltpu.VMEM((rows, D_MODEL), BF16),
        pltpu.VMEM((rows, WA), F32),
        pltpu.VMEM((rows, WB), F32),
        pltpu.VMEM((rows, WB), F32),
        pltpu.VMEM((rows, WB), F32),
        pltpu.VMEM((rows, WB), F32),
        pltpu.VMEM((rows, WB), F32),
        pltpu.VMEM((rows, WC), F32),
        pltpu.VMEM((kb_rows, WC), BF16),
        pltpu.VMEM((kb_rows, WC), BF16),
        pltpu.VMEM((CONV_HIST + L, WA), F32),
        pltpu.VMEM((HB, DVB, DKB), F32),
        pltpu.VMEM((3, L, WB), F32),
        pltpu.VMEM((rows, WA), BF16),
        pltpu.VMEM((rows, WB), BF16),
        pltpu.VMEM((rows, WC), BF16),
        pltpu.VMEM((rows, D_MODEL), BF16),
    ]


def _layer_weights(l, w_in, conv_dw_w, conv_dw_b, conv_ln_g, conv_ln_b, w_conv_out, hgrn_lb_logits,
                   hgrn_norm_g, w_hgrn_out, w_attn_out, w_mix_out, g_mix):
    return dict(
        win=w_in[l].astype(BF16), gmix=g_mix[l][None, :], dww=conv_dw_w[l], dwb=conv_dw_b[l][None, :],
        lng=conv_ln_g[l][None, :], lnb=conv_ln_b[l][None, :], wc=w_conv_out[l].astype(BF16),
        lbl=hgrn_lb_logits, ng=hgrn_norm_g[l][None, :], wh=w_hgrn_out[l].astype(BF16),
        wa=w_attn_out[l].astype(BF16), wm=w_mix_out[l].astype(BF16))


def _weight_specs(w, bias):
    names = ("win", "gmix", "dww", "dwb", "lng", "lnb", "wc", "lbl", "ng", "wh")
    args = [w[k] for k in names] + [bias, w["wa"], w["wm"]]
    return args, [_const_spec(a.shape) for a in args]


def _mixer_prompt(layer, x, w, bias):
    n, t_len, _ = x.shape
    n_tiles = t_len // TILE
    wargs, wspecs = _weight_specs(w, bias)
    out_shape = (
        jax.ShapeDtypeStruct((n, t_len, D_MODEL), F32),
        jax.ShapeDtypeStruct((n, CONV_W - 1, WA), F32),
        jax.ShapeDtypeStruct((n, HB, DKB, DVB), F32),
        jax.ShapeDtypeStruct((n, BAND_PAST, WC), F32),
        jax.ShapeDtypeStruct((n, BAND_PAST, WC), F32),
    )
    out_specs = (
        pl.BlockSpec((1, TILE, D_MODEL), lambda i, t: (i, t, 0)),
        pl.BlockSpec((1, CONV_W - 1, WA), lambda i, t: (i, 0, 0)),
        pl.BlockSpec((1, HB, DKB, DVB), lambda i, t: (i, 0, 0, 0)),
        pl.BlockSpec((1, BAND_PAST, WC), lambda i, t: (i, 0, 0)),
        pl.BlockSpec((1, BAND_PAST, WC), lambda i, t: (i, 0, 0)),
    )
    return pl.pallas_call(
        functools.partial(_mixer_prompt_kernel, layer, n_tiles),
        grid=(n, n_tiles),
        in_specs=[pl.BlockSpec((1, TILE, D_MODEL), lambda i, t: (i, t, 0))] + wspecs,
        out_specs=out_specs,
        out_shape=out_shape,
        scratch_shapes=_mixer_scratch(TILE, CHUNK, TILE - CHUNK + KEYS),
        compiler_params=pltpu.CompilerParams(dimension_semantics=("arbitrary", "arbitrary"),
                                             vmem_limit_bytes=VMEM_LIMIT),
        name=f"mixer_prompt_l{layer}",
    )(x, *wargs)


def _mixer_sample(layer, x, conv_state, s0, cache_k, cache_v, w, bias):
    n, L, _ = x.shape
    rows = n * L
    n_cache = cache_k.shape[1]
    wargs, wspecs = _weight_specs(w, bias)
    conv_pad = jnp.pad(conv_state, ((0, 0), (CONV_HIST - (CONV_W - 1), 0), (0, 0)))
    out_shape = (
        jax.ShapeDtypeStruct((rows, D_MODEL), F32),
        jax.ShapeDtypeStruct((n, CONV_W - 1, WA), F32),
        jax.ShapeDtypeStruct((n, HB, DKB, DVB), F32),
        jax.ShapeDtypeStruct((rows, WC), F32),
        jax.ShapeDtypeStruct((rows, WC), F32),
    )
    out_specs = (
        pl.BlockSpec((rows, D_MODEL), lambda i: (0, 0)),
        pl.BlockSpec((1, CONV_W - 1, WA), lambda i: (i, 0, 0)),
        pl.BlockSpec((1, HB, DKB, DVB), lambda i: (i, 0, 0, 0)),
        pl.BlockSpec((rows, WC), lambda i: (0, 0)),
        pl.BlockSpec((rows, WC), lambda i: (0, 0)),
    )
    in_specs = [
        pl.BlockSpec((rows, D_MODEL), lambda i: (0, 0)),
        pl.BlockSpec((1, CONV_HIST, WA), lambda i: (i, 0, 0)),
        pl.BlockSpec((1, HB, DKB, DVB), lambda i: (i, 0, 0, 0)),
        pl.BlockSpec((1, n_cache, WC), lambda i: (i, 0, 0)),
        pl.BlockSpec((1, n_cache, WC), lambda i: (i, 0, 0)),
    ] + wspecs
    return pl.pallas_call(
        functools.partial(_mixer_sample_kernel, layer, n, L),
        grid=(n,),
        in_specs=in_specs,
        out_specs=out_specs,
        out_shape=out_shape,
        scratch_shapes=_mixer_scratch(rows, L, KEYS) + [pltpu.VMEM((rows, WC), BF16), pltpu.VMEM((rows, WC), BF16)],
        compiler_params=pltpu.CompilerParams(dimension_semantics=("arbitrary",),
                                             vmem_limit_bytes=VMEM_LIMIT),
        name=f"mixer_sample_l{layer}",
    )(x.reshape(rows, D_MODEL), conv_pad, s0, cache_k, cache_v, *wargs)


def _ffn(layer, final, h, state, n_seq, w_ffn_up, ffn_dw_w, w_ffn_down, g_ffn, g_final):
    groups, rows, _ = h.shape
    rows_t = min(rows, TILE)
    n_tiles = rows // rows_t
    keep = 2 * n_seq
    hist = max(SUBLANES, keep)
    wu = w_ffn_up[layer].astype(BF16).reshape(D_MODEL, 2 * N_FF_BLK, FF_BLK).transpose(1, 0, 2)
    dw = ffn_dw_w[layer].reshape(FFN_CONV_W, 2 * N_FF_BLK, FF_BLK).transpose(1, 0, 2)
    wd = w_ffn_down[layer].astype(BF16).reshape(N_FF_BLK, FF_BLK, D_MODEL)
    st = state.reshape(groups, keep, 2 * N_FF_BLK, FF_BLK).transpose(0, 2, 1, 3)
    args = [h, st, g_ffn[layer][None, :], wu[:N_FF_BLK], wu[N_FF_BLK:], dw[:N_FF_BLK], dw[N_FF_BLK:], wd,
            g_final[None, :]]
    in_specs = [
        pl.BlockSpec((1, rows_t, D_MODEL), lambda i, t: (i, t, 0)),
        pl.BlockSpec((1, 2 * N_FF_BLK, keep, FF_BLK), lambda i, t: (i, 0, 0, 0)),
    ] + [_const_spec(a.shape) for a in args[2:]]
    out, new_st = pl.pallas_call(
        functools.partial(_ffn_kernel, final, n_tiles, n_seq),
        grid=(groups, n_tiles),
        in_specs=in_specs,
        out_specs=(pl.BlockSpec((1, rows_t, D_MODEL), lambda i, t: (i, t, 0)),
                   pl.BlockSpec((1, 2 * N_FF_BLK, keep, FF_BLK), lambda i, t: (i, 0, 0, 0))),
        out_shape=(jax.ShapeDtypeStruct((groups, rows, D_MODEL), F32),
                   jax.ShapeDtypeStruct((groups, 2 * N_FF_BLK, keep, FF_BLK), F32)),
        scratch_shapes=[pltpu.VMEM((rows_t, D_MODEL), BF16),
                        pltpu.VMEM((rows_t, D_MODEL), F32),
                        pltpu.VMEM((2 * N_FF_BLK, hist, FF_BLK), F32),
                        pltpu.VMEM((2, hist + rows_t, FF_BLK), F32)],
        compiler_params=pltpu.CompilerParams(dimension_semantics=("arbitrary", "arbitrary"),
                                             vmem_limit_bytes=VMEM_LIMIT),
        name=f"ffn_l{layer}_{'s' if n_seq > 1 else 'p'}",
    )(*args)
    return out, new_st.transpose(0, 2, 1, 3).reshape(groups, keep, 2 * D_FF)


def kernel(x_prompt, x_sample, state_conv, state_hgrn, cache_attn_k, cache_attn_v, state_ffn, w_in, conv_dw_w, conv_dw_b, conv_ln_g, conv_ln_b, w_conv_out, hgrn_lb_logits, hgrn_norm_g, w_hgrn_out, attn_rel_bias, w_attn_out, w_mix_out, g_mix, w_ffn_up, ffn_dw_w, w_ffn_down, g_ffn, g_final):
    depth = w_in.shape[0]
    n_p, t_p, _ = x_prompt.shape
    n_s, t_s, _ = x_sample.shape
    n_cache = cache_attn_k.shape[2]
    assert t_p % TILE == 0 and TILE == BAND_PAST and n_cache == min(BAND_PAST, PAST_LEN)
    assert n_cache + t_s <= KEYS and t_s % SUB == 0 and t_s >= CONV_W - 1

    xp, xs = x_prompt, x_sample
    outs = [[] for _ in range(10)]
    for l in range(depth):
        w = _layer_weights(l, w_in, conv_dw_w, conv_dw_b, conv_ln_g, conv_ln_b, w_conv_out, hgrn_lb_logits,
                           hgrn_norm_g, w_hgrn_out, w_attn_out, w_mix_out, g_mix)
        final = l == depth - 1
        bias_p = _bias_table(attn_rel_bias[l], CHUNK, BAND_PAST + CHUNK)
        hp, c1, s1, k1, v1 = _mixer_prompt(l, xp, w, bias_p)
        zero_ffn = jnp.zeros((n_p, FFN_CONV_W - 1, 2 * D_FF), F32)
        xp, f1 = _ffn(l, final, hp, zero_ffn, 1, w_ffn_up, ffn_dw_w, w_ffn_down, g_ffn, g_final)
        bias_s = _bias_table(attn_rel_bias[l], t_s, n_cache + t_s)
        hs, c2, s2, k2, v2 = _mixer_sample(
            l, xs, state_conv[l], state_hgrn[l], cache_attn_k[l].reshape(n_s, n_cache, WC),
            cache_attn_v[l].reshape(n_s, n_cache, WC), w, bias_s)
        hs_tm = hs.reshape(n_s, t_s, D_MODEL).transpose(1, 0, 2).reshape(1, t_s * n_s, D_MODEL)
        st_tm = state_ffn[l].transpose(1, 0, 2).reshape(1, (FFN_CONV_W - 1) * n_s, 2 * D_FF)
        ys_tm, f2_tm = _ffn(l, final, hs_tm, st_tm, n_s, w_ffn_up, ffn_dw_w, w_ffn_down, g_ffn, g_final)
        xs = ys_tm.reshape(t_s, n_s, D_MODEL).transpose(1, 0, 2)
        f2 = f2_tm.reshape(FFN_CONV_W - 1, n_s, 2 * D_FF).transpose(1, 0, 2)
        for lst, val in zip(outs, (c1, c2, s1, s2,
                                   k1.reshape(n_p, BAND_PAST, HC, DHC), v1.reshape(n_p, BAND_PAST, HC, DHC),
                                   k2.reshape(n_s, t_s, HC, DHC), v2.reshape(n_s, t_s, HC, DHC), f1, f2)):
            lst.append(val)
    return (xp, xs) + tuple(jnp.stack(o) for o in outs)
```

```python
import functools

import jax
import jax.numpy as jnp
from jax import lax
from jax.experimental import pallas as pl
from jax.experimental.pallas import tpu as pltpu

F32 = jnp.float32
BF16 = jnp.bfloat16

D_MODEL = 1024
CHUNK = 64
WA = 256
CONV_W = 31
HB = 4
DKB = 128
DVB = 128
WB = HB * DVB
HC = 4
DHC = 64
WC = HC * DHC
BAND_CHUNKS = 8
BAND_PAST = BAND_CHUNKS * CHUNK
REL_CLIP = 128
D_FF = 2816
FFN_CONV_W = 3
EPS = 1e-6
PAST_LEN = 2048

C_A = 0
C_BQ = 2 * WA
C_BF = C_BQ + HB * DKB
C_BI = C_BF + HB * DKB
C_BG = C_BI + WB
C_C = C_BG + WB
C_GATE = C_C + 3 * WC
N_IN = C_GATE + 3 * D_MODEL

TILE = 512
SUB = 8
KEYS = 640
CONV_HIST = 32
FF_BLK = 256
N_FF_BLK = D_FF // FF_BLK
NEG = -1e30
SUBLANES = 8
V7X_VMEM_BYTES = 64 * 1024 * 1024
VMEM_LIMIT = V7X_VMEM_BYTES - 8 * 1024 * 1024


def _dot(a, b):
    return jnp.dot(a, b, preferred_element_type=F32)


def _dot_nt(a, b):
    return lax.dot_general(a, b, (((1,), (1,)), ((), ())), preferred_element_type=F32)


def _sig(x):
    return jax.nn.sigmoid(x)


def _silu(x):
    return x * jax.nn.sigmoid(x)


def _rms(x, g):
    return x * lax.rsqrt(jnp.mean(x * x, axis=-1, keepdims=True) + EPS) * g


def _lower_bound(lbl_ref, layer):
    lg = lbl_ref[...]
    e = jnp.exp(lg - jnp.max(lg, axis=0, keepdims=True))
    p = e / jnp.sum(e, axis=0, keepdims=True)
    return jnp.sum(p[0:layer + 1], axis=0, keepdims=True) - p[0:1]


def _project_in(x, L, win_ref, gmix_ref, lb, xn_s, u_s, q_s, k_s, lf_s, v_s, g_s, aq_s):
    xn = _rms(x, gmix_ref[...]).astype(BF16)
    xn_s[...] = xn
    a = _dot(xn, win_ref[:, C_A:C_A + 2 * WA])
    u_s[...] = a[:, :WA] * _sig(a[:, WA:])
    q_s[...] = _silu(_dot(xn, win_ref[:, C_BQ:C_BF]))
    zf = _dot(xn, win_ref[:, C_BF:C_BI])
    lf_s[...] = _cumsum_blocks(jnp.log(lb + (1.0 - lb) * _sig(zf)), L)
    k_s[...] = (1.0 - lb) * _sig(-zf)
    v_s[...] = _dot(xn, win_ref[:, C_BI:C_BG])
    g_s[...] = _silu(_dot(xn, win_ref[:, C_BG:C_C]))
    zc = _dot(xn, win_ref[:, C_C:C_GATE])
    aq_s[...] = zc[:, :WC] * (DHC ** -0.5)
    return zc[:, WC:2 * WC], zc[:, 2 * WC:]


def _conv_chunk(ext_s, sh_s, dww_ref, dwb_ref, lng_ref, lnb_ref, L):
    lo = CONV_HIST - (CONV_W - 1)
    acc = jnp.zeros((L, WA), F32) + dwb_ref[...]
    for r in range(SUBLANES):
        taps = [p - lo for p in range(r, lo + CONV_W, SUBLANES) if p >= lo]
        n = L + SUBLANES * ((lo + taps[-1]) // SUBLANES)
        if r:
            sh_s[r - 1, 0:n, :] = ext_s[r:r + n, :]
        for j in taps:
            a = (lo + j) // SUBLANES
            src = sh_s[r - 1, SUBLANES * a:SUBLANES * a + L, :] if r else ext_s[SUBLANES * a:SUBLANES * a + L, :]
            acc = acc + src * dww_ref[j:j + 1, :]
    mu = jnp.mean(acc, axis=-1, keepdims=True)
    xc = acc - mu
    var = jnp.mean(xc * xc, axis=-1, keepdims=True)
    y = xc * lax.rsqrt(var + EPS) * lng_ref[...] + lnb_ref[...]
    return _silu(y)


def _cumsum_blocks(x, L):
    rowmod = lax.broadcasted_iota(jnp.int32, x.shape, 0) & (L - 1)
    step = 1
    while step < L:
        x = x + jnp.where(rowmod >= step, pltpu.roll(x, step, axis=0), 0.0)
        step *= 2
    return x


def _hgrn_chunk(q_s, k_s, lf_s, v_s, g_s, ng_ref, r0, L, S_s, c_s, oB_s):
    nb = L // SUB
    rows = pl.ds(r0, L)
    b_s, kc_s, vc_s = c_s.at[0], c_s.at[1], c_s.at[2]
    b_s[...] = lf_s[rows, :]
    kc_s[...] = k_s[rows, :]
    vc_s[...] = v_s[rows, :]
    trow = lax.broadcasted_iota(jnp.int32, (SUB, DKB), 0)
    colblk = lax.shift_right_logical(lax.broadcasted_iota(jnp.int32, (SUB, L), 1), SUB.bit_length() - 1)
    for h in range(HB):
        ls = slice(h * DKB, (h + 1) * DKB)
        b = b_s[:, ls]
        q = q_s[rows, ls]
        k = kc_s[:, ls]
        v = vc_s[:, ls]
        st = S_s[h]

        def row(r):
            return jnp.broadcast_to(b_s[r:r + 1, ls], (SUB, DKB))

        first = [row(SUB * i) for i in range(nb)]
        nxt = [row(SUB * (j + 1)) for j in range(nb - 1)] + [row(L - 1)]
        r_first = jnp.concatenate(first, axis=0)
        r_next = jnp.concatenate(nxt, axis=0)
        qt = q * jnp.exp(b - r_first)
        kh = k * jnp.exp(r_next - b)
        o = _dot_nt((qt * jnp.exp(r_first)).astype(BF16), st.astype(BF16))
        lhs = []
        for i in range(1, nb):
            qi = qt[i * SUB:(i + 1) * SUB]
            for j in range(i):
                lhs.append(qi if j == i - 1 else qi * jnp.exp(first[i] - nxt[j]))
        pair = _dot_nt(jnp.concatenate(lhs, axis=0).astype(BF16), kh.astype(BF16))
        a_rows = [jnp.zeros((SUB, L), F32)]
        p = 0
        for i in range(1, nb):
            acc = jnp.zeros((SUB, L), F32)
            for j in range(i):
                acc = jnp.where(colblk == j, pair[p * SUB:(p + 1) * SUB], acc)
                p += 1
            a_rows.append(acc)
        a = jnp.concatenate(a_rows, axis=0)
        o = o + _dot(a.astype(BF16), v.astype(BF16))
        od = []
        for i in range(nb):
            qb = q[i * SUB:(i + 1) * SUB]
            bb = b[i * SUB:(i + 1) * SUB]
            acc = jnp.zeros((SUB, DVB), F32)
            for s in range(SUB):
                g = i * SUB + s
                bg = b_s[g:g + 1, ls]
                kg = kc_s[g:g + 1, ls]
                vg = vc_s[g:g + 1, ls]
                e = jnp.exp(jnp.where(trow >= s, bb - bg, NEG))
                col = jnp.sum(qb * kg * e, axis=1, keepdims=True)
                acc = acc + col * vg
            od.append(acc)
        o = o + jnp.concatenate(od, axis=0)
        bl = b_s[L - 1:L, ls]
        k_end = (kh * jnp.exp(bl - r_next)).astype(BF16)
        S_s[h] = st * jnp.exp(bl) + _dot(v.T.astype(BF16), k_end)
        on = o * lax.rsqrt(jnp.mean(o * o, axis=-1, keepdims=True) + EPS)
        oB_s[rows, ls] = (on * ng_ref[:, ls] * g_s[rows, ls]).astype(BF16)


def _attn_chunk(q, kband, vband, bias_ref, valid):
    L = q.shape[0]
    low = lax.broadcasted_iota(jnp.int32, (1, 2 * DHC), 1) < DHC
    outs = []
    for p in range(HC // 2):
        ls = slice(p * 2 * DHC, (p + 1) * 2 * DHC)
        qp = q[:, ls]
        q2 = jnp.concatenate([jnp.where(low, qp, 0.0), jnp.where(low, 0.0, qp)], axis=0).astype(BF16)
        s = _dot_nt(q2, kband[:, ls]) + bias_ref[p]
        if valid is not None:
            s = jnp.where(valid, s, NEG)
        e = jnp.exp(s - jnp.max(s, axis=-1, keepdims=True))
        den = jnp.sum(e, axis=-1, keepdims=True)
        pv = _dot(e.astype(BF16), vband[:, ls]) / den
        outs.append(jnp.where(low, pv[0:L], pv[L:2 * L]))
    return jnp.concatenate(outs, axis=1)


def _merge_out(x, xn_s, hA_s, oB_s, oC_s, m_s, win_ref, wc_ref, wh_ref, wa_ref, wm_ref):
    half = D_MODEL // 2
    for cb in range(2):
        cs = slice(cb * half, (cb + 1) * half)
        m = None
        for k, (src, w_ref) in enumerate(((hA_s, wc_ref), (oB_s, wh_ref), (oC_s, wa_ref))):
            g0 = C_GATE + k * D_MODEL + cb * half
            gate = _sig(_dot(xn_s[...], win_ref[:, g0:g0 + half]))
            term = gate * _dot(src[...], w_ref[:, cs])
            m = term if m is None else m + term
        m_s[:, cs] = m.astype(BF16)
    return x + _dot(m_s[...], wm_ref[...])


def _mixer_prompt_kernel(layer, n_tiles,
                         x_ref, win_ref, gmix_ref, dww_ref, dwb_ref, lng_ref, lnb_ref, wc_ref,
                         lbl_ref, ng_ref, wh_ref, bias_ref, wa_ref, wm_ref,
                         h_ref, nconv_ref, ns_ref, nk_ref, nv_ref,
                         xn_s, u_s, q_s, k_s, lf_s, v_s, g_s, aq_s, kb_s, vb_s, ext_s, sh_s, S_s, b_s,
                         hA_s, oB_s, oC_s, m_s):
    t = pl.program_id(1)

    @pl.when(t == 0)
    def _():
        ext_s[0:CONV_HIST, :] = jnp.zeros((CONV_HIST, WA), F32)
        S_s[...] = jnp.zeros_like(S_s)
        kb_s[...] = jnp.zeros_like(kb_s)
        vb_s[...] = jnp.zeros_like(vb_s)

    @pl.when(t > 0)
    def _():
        kb_s[0:TILE, :] = kb_s[TILE:2 * TILE, :]
        vb_s[0:TILE, :] = vb_s[TILE:2 * TILE, :]

    x = x_ref[0]
    lb = _lower_bound(lbl_ref, layer)
    ck, cv = _project_in(x, CHUNK, win_ref, gmix_ref, lb, xn_s, u_s, q_s, k_s, lf_s, v_s, g_s, aq_s)
    kb_s[TILE:2 * TILE, :] = ck.astype(BF16)
    vb_s[TILE:2 * TILE, :] = cv.astype(BF16)

    @pl.when(t == n_tiles - 1)
    def _():
        nk_ref[0] = ck
        nv_ref[0] = cv

    col = lax.broadcasted_iota(jnp.int32, (1, KEYS), 1)

    def chunk(c, carry):
        r0 = pl.multiple_of(c * CHUNK, CHUNK)
        rows = pl.ds(r0, CHUNK)
        ext_s[CONV_HIST:CONV_HIST + CHUNK, :] = u_s[rows, :]
        hA_s[rows, :] = _conv_chunk(ext_s, sh_s, dww_ref, dwb_ref, lng_ref, lnb_ref, CHUNK).astype(BF16)
        ext_s[0:CONV_HIST, :] = ext_s[CHUNK:CHUNK + CONV_HIST, :]
        _hgrn_chunk(q_s, k_s, lf_s, v_s, g_s, ng_ref, r0, CHUNK, S_s, b_s, oB_s)
        valid = col >= BAND_PAST - r0 - t * TILE
        keys = pl.ds(r0, KEYS)
        oC_s[rows, :] = _attn_chunk(aq_s[rows, :], kb_s[keys, :], vb_s[keys, :], bias_ref, valid).astype(BF16)
        return carry

    lax.fori_loop(0, TILE // CHUNK, chunk, 0)

    h_ref[0] = _merge_out(x, xn_s, hA_s, oB_s, oC_s, m_s, win_ref, wc_ref, wh_ref, wa_ref, wm_ref)

    @pl.when(t == n_tiles - 1)
    def _():
        nconv_ref[0] = ext_s[CONV_HIST - (CONV_W - 1):CONV_HIST, :]
        for h in range(HB):
            ns_ref[0, h] = S_s[h].T


def _mixer_sample_kernel(layer, n_seq, L,
                         x_ref, conv_ref, s0_ref, cak_ref, cav_ref,
                         win_ref, gmix_ref, dww_ref, dwb_ref, lng_ref, lnb_ref, wc_ref,
                         lbl_ref, ng_ref, wh_ref, bias_ref, wa_ref, wm_ref,
                         h_ref, nconv_ref, ns_ref, nk_ref, nv_ref,
                         xn_s, u_s, q_s, k_s, lf_s, v_s, g_s, aq_s, kb_s, vb_s, ext_s, sh_s, S_s, b_s,
                         hA_s, oB_s, oC_s, m_s, kn_s, vn_s):
    n = pl.program_id(0)
    n_cache = cak_ref.shape[1]

    @pl.when(n == 0)
    def _():
        lb = _lower_bound(lbl_ref, layer)
        ck, cv = _project_in(x_ref[...], L, win_ref, gmix_ref, lb, xn_s, u_s, q_s, k_s, lf_s, v_s, g_s, aq_s)
        nk_ref[...] = ck
        nv_ref[...] = cv
        kn_s[...] = ck.astype(BF16)
        vn_s[...] = cv.astype(BF16)
        kb_s[...] = jnp.zeros_like(kb_s)
        vb_s[...] = jnp.zeros_like(vb_s)

    r0 = pl.multiple_of(n * L, L)
    rows = pl.ds(r0, L)
    ext_s[0:CONV_HIST, :] = conv_ref[0]
    ext_s[CONV_HIST:CONV_HIST + L, :] = u_s[rows, :]
    hA_s[rows, :] = _conv_chunk(ext_s, sh_s, dww_ref, dwb_ref, lng_ref, lnb_ref, L).astype(BF16)
    nconv_ref[0] = ext_s[L + CONV_HIST - (CONV_W - 1):L + CONV_HIST, :]
    for h in range(HB):
        S_s[h] = s0_ref[0, h].T
    _hgrn_chunk(q_s, k_s, lf_s, v_s, g_s, ng_ref, r0, L, S_s, b_s, oB_s)
    for h in range(HB):
        ns_ref[0, h] = S_s[h].T
    kb_s[0:n_cache, :] = cak_ref[0].astype(BF16)
    vb_s[0:n_cache, :] = cav_ref[0].astype(BF16)
    kb_s[n_cache:n_cache + L, :] = kn_s[rows, :]
    vb_s[n_cache:n_cache + L, :] = vn_s[rows, :]
    oC_s[rows, :] = _attn_chunk(aq_s[rows, :], kb_s[...], vb_s[...], bias_ref, None).astype(BF16)

    @pl.when(n == n_seq - 1)
    def _():
        h_ref[...] = _merge_out(x_ref[...], xn_s, hA_s, oB_s, oC_s, m_s, win_ref, wc_ref, wh_ref, wa_ref, wm_ref)


def _ffn_kernel(final, n_tiles, n_seq,
                h_ref, st_ref, gffn_ref, wua_ref, wub_ref, dwa_ref, dwb_ref, wd_ref, gfin_ref,
                out_ref, nffn_ref,
                hn_s, acc_s, tail_s, ext_s):
    t = pl.program_id(1)
    rows_t = h_ref.shape[1]
    keep = 2 * n_seq
    hist = tail_s.shape[1]

    @pl.when(t == 0)
    def _():
        tail_s[...] = jnp.zeros_like(tail_s)
        for jj in range(2 * N_FF_BLK):
            tail_s[jj, hist - keep:hist, :] = st_ref[0, jj]

    h = h_ref[0]
    hn_s[...] = _rms(h, gffn_ref[...]).astype(BF16)
    acc_s[...] = jnp.zeros_like(acc_s)

    def up(j, slot):
        hn = hn_s[...]
        for ab, (w_ref, jj) in enumerate(((wua_ref, j), (wub_ref, N_FF_BLK + j))):
            u = _dot(hn, w_ref[j])
            ext_s[slot, ab, 0:hist, :] = tail_s[jj]
            ext_s[slot, ab, hist:hist + rows_t, :] = u
            tail_s[jj] = u[rows_t - hist:rows_t]

    def conv3(slot, ab, w):
        u = ext_s[slot, ab, hist:hist + rows_t, :]
        u1 = ext_s[slot, ab, hist - n_seq:hist - n_seq + rows_t, :]
        u2 = ext_s[slot, ab, hist - keep:hist - keep + rows_t, :]
        return u * w[2:3] + u1 * w[1:2] + u2 * w[0:1]

    def down(j, slot):
        ca = conv3(slot, 0, dwa_ref[j])
        cb = conv3(slot, 1, dwb_ref[j])
        acc_s[...] += _dot((_silu(ca) * cb).astype(BF16), wd_ref[j])

    assert N_FF_BLK % 2 == 1
    up(0, 0)

    def block(i, carry):
        j = 2 * i
        up(j + 1, 1)
        down(j, 0)
        up(j + 2, 0)
        down(j + 1, 1)
        return carry

    lax.fori_loop(0, N_FF_BLK // 2, block, 0)
    down(N_FF_BLK - 1, 0)
    y = h + acc_s[...]
    if final:
        y = _rms(y, gfin_ref[...])
    out_ref[0] = y

    @pl.when(t == n_tiles - 1)
    def _():
        for jj in range(2 * N_FF_BLK):
            nffn_ref[0, jj] = tail_s[jj, hist - keep:hist, :]


def _const_spec(shape):
    zeros = (0,) * len(shape)
    return pl.BlockSpec(shape, lambda *_: zeros, pipeline_mode=pl.Buffered(1))


def _bias_table(tab, n_q, n_k):
    span = n_q + n_k - 1
    n_flat = BAND_PAST + n_q - 1 - REL_CLIP
    lo = BAND_PAST + n_q - 1 - (span - 1) + REL_CLIP
    assert lo >= 0 and n_flat >= 0
    g = jnp.concatenate([jnp.broadcast_to(tab[:, 2 * REL_CLIP:], (HC, n_flat)),
                         tab[:, lo:2 * REL_CLIP + 1][:, ::-1]], axis=1)
    rows = [g[:, n_q - 1 - i:n_q - 1 - i + n_k] for i in range(n_q)]
    bias = jnp.stack(rows, axis=1)
    bias = jnp.pad(bias, ((0, 0), (0, 0), (0, KEYS - n_k)), constant_values=NEG)
    return bias.reshape(HC // 2, 2 * n_q, KEYS)


def _mixer_scratch(rows, L, kb_rows):
    return [
        pltpu.VMEM((rows, D_MODEL), BF16),
        pltpu.VMEM((rows, WA), F32),
        pltpu.VMEM((rows, WB), F32),
        pltpu.VMEM((rows, WB), F32),
        pltpu.VMEM((rows, WB), F32),
        pltpu.VMEM((rows, WB), F32),
        pltpu.VMEM((rows, WB), F32),
        pltpu.VMEM((rows, WC), F32),
        pltpu.VMEM((kb_rows, WC), BF16),
        pltpu.VMEM((kb_rows, WC), BF16),
        pltpu.VMEM((CONV_HIST + L, WA), F32),
        pltpu.VMEM((SUBLANES - 1, CONV_HIST - SUBLANES + L, WA), F32),
        pltpu.VMEM((HB, DVB, DKB), F32),
        pltpu.VMEM((3, L, WB), F32),
        pltpu.VMEM((rows, WA), BF16),
        pltpu.VMEM((rows, WB), BF16),
        pltpu.VMEM((rows, WC), BF16),
        pltpu.VMEM((rows, D_MODEL), BF16),
    ]


def _layer_weights(l, w_in, conv_dw_w, conv_dw_b, conv_ln_g, conv_ln_b, w_conv_out, hgrn_lb_logits,
                   hgrn_norm_g, w_hgrn_out, w_attn_out, w_mix_out, g_mix):
    return dict(
        win=w_in[l].astype(BF16), gmix=g_mix[l][None, :], dww=conv_dw_w[l], dwb=conv_dw_b[l][None, :],
        lng=conv_ln_g[l][None, :], lnb=conv_ln_b[l][None, :], wc=w_conv_out[l].astype(BF16),
        lbl=hgrn_lb_logits, ng=hgrn_norm_g[l][None, :], wh=w_hgrn_out[l].astype(BF16),
        wa=w_attn_out[l].astype(BF16), wm=w_mix_out[l].astype(BF16))


def _weight_specs(w, bias):
    names = ("win", "gmix", "dww", "dwb", "lng", "lnb", "wc", "lbl", "ng", "wh")
    args = [w[k] for k in names] + [bias, w["wa"], w["wm"]]
    return args, [_const_spec(a.shape) for a in args]


def _mixer_prompt(layer, x, w, bias):
    n, t_len, _ = x.shape
    n_tiles = t_len // TILE
    wargs, wspecs = _weight_specs(w, bias)
    out_shape = (
        jax.ShapeDtypeStruct((n, t_len, D_MODEL), F32),
        jax.ShapeDtypeStruct((n, CONV_W - 1, WA), F32),
        jax.ShapeDtypeStruct((n, HB, DKB, DVB), F32),
        jax.ShapeDtypeStruct((n, BAND_PAST, WC), F32),
        jax.ShapeDtypeStruct((n, BAND_PAST, WC), F32),
    )
    out_specs = (
        pl.BlockSpec((1, TILE, D_MODEL), lambda i, t: (i, t, 0)),
        pl.BlockSpec((1, CONV_W - 1, WA), lambda i, t: (i, 0, 0)),
        pl.BlockSpec((1, HB, DKB, DVB), lambda i, t: (i, 0, 0, 0)),
        pl.BlockSpec((1, BAND_PAST, WC), lambda i, t: (i, 0, 0)),
        pl.BlockSpec((1, BAND_PAST, WC), lambda i, t: (i, 0, 0)),
    )
    return pl.pallas_call(
        functools.partial(_mixer_prompt_kernel, layer, n_tiles),
        grid=(n, n_tiles),
        in_specs=[pl.BlockSpec((1, TILE, D_MODEL), lambda i, t: (i, t, 0))] + wspecs,
        out_specs=out_specs,
        out_shape=out_shape,
        scratch_shapes=_mixer_scratch(TILE, CHUNK, TILE - CHUNK + KEYS),
        compiler_params=pltpu.CompilerParams(dimension_semantics=("arbitrary", "arbitrary"),
                                             vmem_limit_bytes=VMEM_LIMIT),
        name=f"mixer_prompt_l{layer}",
    )(x, *wargs)


def _mixer_sample(layer, x, conv_state, s0, cache_k, cache_v, w, bias):
    n, L, _ = x.shape
    rows = n * L
    n_cache = cache_k.shape[1]
    wargs, wspecs = _weight_specs(w, bias)
    conv_pad = jnp.pad(conv_state, ((0, 0), (CONV_HIST - (CONV_W - 1), 0), (0, 0)))
    out_shape = (
        jax.ShapeDtypeStruct((rows, D_MODEL), F32),
        jax.ShapeDtypeStruct((n, CONV_W - 1, WA), F32),
        jax.ShapeDtypeStruct((n, HB, DKB, DVB), F32),
        jax.ShapeDtypeStruct((rows, WC), F32),
        jax.ShapeDtypeStruct((rows, WC), F32),
    )
    out_specs = (
        pl.BlockSpec((rows, D_MODEL), lambda i: (0, 0)),
        pl.BlockSpec((1, CONV_W - 1, WA), lambda i: (i, 0, 0)),
        pl.BlockSpec((1, HB, DKB, DVB), lambda i: (i, 0, 0, 0)),
        pl.BlockSpec((rows, WC), lambda i: (0, 0)),
        pl.BlockSpec((rows, WC), lambda i: (0, 0)),
    )
    in_specs = [
        pl.BlockSpec((rows, D_MODEL), lambda i: (0, 0)),
        pl.BlockSpec((1, CONV_HIST, WA), lambda i: (i, 0, 0)),
        pl.BlockSpec((1, HB, DKB, DVB), lambda i: (i, 0, 0, 0)),
        pl.BlockSpec((1, n_cache, WC), lambda i: (i, 0, 0)),
        pl.BlockSpec((1, n_cache, WC), lambda i: (i, 0, 0)),
    ] + wspecs
    return pl.pallas_call(
        functools.partial(_mixer_sample_kernel, layer, n, L),
        grid=(n,),
        in_specs=in_specs,
        out_specs=out_specs,
        out_shape=out_shape,
        scratch_shapes=_mixer_scratch(rows, L, KEYS) + [pltpu.VMEM((rows, WC), BF16), pltpu.VMEM((rows, WC), BF16)],
        compiler_params=pltpu.CompilerParams(dimension_semantics=("arbitrary",),
                                             vmem_limit_bytes=VMEM_LIMIT),
        name=f"mixer_sample_l{layer}",
    )(x.reshape(rows, D_MODEL), conv_pad, s0, cache_k, cache_v, *wargs)


def _ffn(layer, final, h, state, n_seq, w_ffn_up, ffn_dw_w, w_ffn_down, g_ffn, g_final):
    groups, rows, _ = h.shape
    rows_t = min(rows, TILE)
    n_tiles = rows // rows_t
    keep = 2 * n_seq
    hist = max(SUBLANES, keep)
    wu = w_ffn_up[layer].astype(BF16).reshape(D_MODEL, 2 * N_FF_BLK, FF_BLK).transpose(1, 0, 2)
    dw = ffn_dw_w[layer].reshape(FFN_CONV_W, 2 * N_FF_BLK, FF_BLK).transpose(1, 0, 2)
    wd = w_ffn_down[layer].astype(BF16).reshape(N_FF_BLK, FF_BLK, D_MODEL)
    st = state.reshape(groups, keep, 2 * N_FF_BLK, FF_BLK).transpose(0, 2, 1, 3)
    args = [h, st, g_ffn[layer][None, :], wu[:N_FF_BLK], wu[N_FF_BLK:], dw[:N_FF_BLK], dw[N_FF_BLK:], wd,
            g_final[None, :]]
    in_specs = [
        pl.BlockSpec((1, rows_t, D_MODEL), lambda i, t: (i, t, 0)),
        pl.BlockSpec((1, 2 * N_FF_BLK, keep, FF_BLK), lambda i, t: (i, 0, 0, 0)),
    ] + [_const_spec(a.shape) for a in args[2:]]
    out, new_st = pl.pallas_call(
        functools.partial(_ffn_kernel, final, n_tiles, n_seq),
        grid=(groups, n_tiles),
        in_specs=in_specs,
        out_specs=(pl.BlockSpec((1, rows_t, D_MODEL), lambda i, t: (i, t, 0)),
                   pl.BlockSpec((1, 2 * N_FF_BLK, keep, FF_BLK), lambda i, t: (i, 0, 0, 0))),
        out_shape=(jax.ShapeDtypeStruct((groups, rows, D_MODEL), F32),
                   jax.ShapeDtypeStruct((groups, 2 * N_FF_BLK, keep, FF_BLK), F32)),
        scratch_shapes=[pltpu.VMEM((rows_t, D_MODEL), BF16),
                        pltpu.VMEM((rows_t, D_MODEL), F32),
                        pltpu.VMEM((2 * N_FF_BLK, hist, FF_BLK), F32),
                        pltpu.VMEM((2, 2, hist + rows_t, FF_BLK), F32)],
        compiler_params=pltpu.CompilerParams(dimension_semantics=("arbitrary", "arbitrary"),
                                             vmem_limit_bytes=VMEM_LIMIT),
        name=f"ffn_l{layer}_{'s' if n_seq > 1 else 'p'}",
    )(*args)
    return out, new_st.transpose(0, 2, 1, 3).reshape(groups, keep, 2 * D_FF)


def kernel(x_prompt, x_sample, state_conv, state_hgrn, cache_attn_k, cache_attn_v, state_ffn, w_in, conv_dw_w, conv_dw_b, conv_ln_g, conv_ln_b, w_conv_out, hgrn_lb_logits, hgrn_norm_g, w_hgrn_out, attn_rel_bias, w_attn_out, w_mix_out, g_mix, w_ffn_up, ffn_dw_w, w_ffn_down, g_ffn, g_final):
    depth = w_in.shape[0]
    n_p, t_p, _ = x_prompt.shape
    n_s, t_s, _ = x_sample.shape
    n_cache = cache_attn_k.shape[2]
    assert t_p % TILE == 0 and TILE == BAND_PAST and n_cache == min(BAND_PAST, PAST_LEN)
    assert n_cache + t_s <= KEYS and t_s % SUB == 0 and t_s >= CONV_W - 1

    xp, xs = x_prompt, x_sample
    outs = [[] for _ in range(10)]
    for l in range(depth):
        w = _layer_weights(l, w_in, conv_dw_w, conv_dw_b, conv_ln_g, conv_ln_b, w_conv_out, hgrn_lb_logits,
                           hgrn_norm_g, w_hgrn_out, w_attn_out, w_mix_out, g_mix)
        final = l == depth - 1
        bias_p = _bias_table(attn_rel_bias[l], CHUNK, BAND_PAST + CHUNK)
        hp, c1, s1, k1, v1 = _mixer_prompt(l, xp, w, bias_p)
        zero_ffn = jnp.zeros((n_p, FFN_CONV_W - 1, 2 * D_FF), F32)
        xp, f1 = _ffn(l, final, hp, zero_ffn, 1, w_ffn_up, ffn_dw_w, w_ffn_down, g_ffn, g_final)
        bias_s = _bias_table(attn_rel_bias[l], t_s, n_cache + t_s)
        hs, c2, s2, k2, v2 = _mixer_sample(
            l, xs, state_conv[l], state_hgrn[l], cache_attn_k[l].reshape(n_s, n_cache, WC),
            cache_attn_v[l].reshape(n_s, n_cache, WC), w, bias_s)
        hs_tm = hs.reshape(n_s, t_s, D_MODEL).transpose(1, 0, 2).reshape(1, t_s * n_s, D_MODEL)
        st_tm = state_ffn[l].transpose(1, 0, 2).reshape(1, (FFN_CONV_W - 1) * n_s, 2 * D_FF)
        ys_tm, f2_tm = _ffn(l, final, hs_tm, st_tm, n_s, w_ffn_up, ffn_dw_w, w_ffn_down, g_ffn, g_final)
        xs = ys_tm.reshape(t_s, n_s, D_MODEL).transpose(1, 0, 2)
        f2 = f2_tm.reshape(FFN_CONV_W - 1, n_s, 2 * D_FF).transpose(1, 0, 2)
        for lst, val in zip(outs, (c1, c2, s1, s2,
                                   k1.reshape(n_p, BAND_PAST, HC, DHC), v1.reshape(n_p, BAND_PAST, HC, DHC),
                                   k2.reshape(n_s, t_s, HC, DHC), v2.reshape(n_s, t_s, HC, DHC), f1, f2)):
            lst.append(val)
    return (xp, xs) + tuple(jnp.stack(o) for o in outs)
```

```python
import functools
import math

import jax
import jax.numpy as jnp
from jax import lax
from jax.experimental import pallas as pl
from jax.experimental.pallas import tpu as pltpu

F32 = jnp.float32
BF16 = jnp.bfloat16

D_MODEL = 1024
CHUNK = 64
WA = 256
CONV_W = 31
HB = 4
DKB = 128
DVB = 128
WB = HB * DVB
HC = 4
DHC = 64
WC = HC * DHC
BAND_CHUNKS = 8
BAND_PAST = BAND_CHUNKS * CHUNK
REL_CLIP = 128
D_FF = 2816
FFN_CONV_W = 3
EPS = 1e-6
PAST_LEN = 2048

C_A = 0
C_BQ = 2 * WA
C_BF = C_BQ + HB * DKB
C_BI = C_BF + HB * DKB
C_BG = C_BI + WB
C_C = C_BG + WB
C_GATE = C_C + 3 * WC
N_IN = C_GATE + 3 * D_MODEL

TILE = 512
SUB = 8
KEYS = 640
BIAS_SPAN = 768
CONV_HIST = 32
FF_BLK = 256
N_FF_BLK = D_FF // FF_BLK
MRG_BLK = 256
N_MRG_BLK = D_MODEL // MRG_BLK
NEG = -1e30
LOG2E = math.log2(math.e)
SUBLANES = 8
V7X_VMEM_BYTES = 64 * 1024 * 1024
VMEM_LIMIT = V7X_VMEM_BYTES - 8 * 1024 * 1024


def _dot(a, b):
    return jnp.dot(a, b, preferred_element_type=F32)


def _dot_nt(a, b):
    return lax.dot_general(a, b, (((1,), (1,)), ((), ())), preferred_element_type=F32)


def _sig(x):
    return jax.nn.sigmoid(x)


def _silu(x):
    return x * jax.nn.sigmoid(x)


def _rms(x, g):
    return x * lax.rsqrt(jnp.mean(x * x, axis=-1, keepdims=True) + EPS) * g


def _lower_bound(lbl_ref, layer):
    lg = lbl_ref[...]
    e = jnp.exp(lg - jnp.max(lg, axis=0, keepdims=True))
    p = e / jnp.sum(e, axis=0, keepdims=True)
    return jnp.sum(p[0:layer + 1], axis=0, keepdims=True) - p[0:1]


def _cumsum_blocks(x, L):
    rowmod = lax.broadcasted_iota(jnp.int32, x.shape, 0) & (L - 1)
    step = 1
    while step < L:
        x = x + jnp.where(rowmod >= step, pltpu.roll(x, step, axis=0), 0.0)
        step *= 2
    return x


def _project_in(x, L, win_ref, gmix_ref, lb, xn_s, u_s, q_s, k_s, lf_s, v_s, g_s, aq_s):
    xn = _rms(x, gmix_ref[...]).astype(BF16)
    xn_s[...] = xn
    a = _dot(xn, win_ref[:, C_A:C_A + 2 * WA])
    u_s[...] = a[:, :WA] * _sig(a[:, WA:])
    q_s[...] = _silu(_dot(xn, win_ref[:, C_BQ:C_BF]))
    zf = _dot(xn, win_ref[:, C_BF:C_BI])
    lf_s[...] = _cumsum_blocks(jnp.log(lb + (1.0 - lb) * _sig(zf)) * LOG2E, L)
    k_s[...] = (1.0 - lb) * _sig(-zf)
    v_s[...] = _dot(xn, win_ref[:, C_BI:C_BG])
    g_s[...] = _silu(_dot(xn, win_ref[:, C_BG:C_C]))
    zc = _dot(xn, win_ref[:, C_C:C_GATE])
    aq_s[...] = zc[:, :WC] * (DHC ** -0.5 * LOG2E)
    return zc[:, WC:2 * WC], zc[:, 2 * WC:]


def _build_bias(gtab_ref, bias_s, L, n_k):
    col = lax.broadcasted_iota(jnp.int32, (L, KEYS), 1)
    for h in range(HC):
        x = jnp.broadcast_to(gtab_ref[h:h + 1, :], (L, BIAS_SPAN))
        y = pltpu.roll(x, BIAS_SPAN - (L - 1), axis=1, stride=1, stride_axis=0)
        bias_s[h // 2, (h % 2) * L:(h % 2 + 1) * L, :] = jnp.where(col < n_k, y[:, :KEYS], NEG)


def _conv_chunk(ext_s, sh_s, dww_ref, dwb_ref, lng_ref, lnb_ref, L):
    lo = CONV_HIST - (CONV_W - 1)
    acc = jnp.zeros((L, WA), F32) + dwb_ref[...]
    for r in range(SUBLANES):
        taps = [p - lo for p in range(r, lo + CONV_W, SUBLANES) if p >= lo]
        n = L + SUBLANES * ((lo + taps[-1]) // SUBLANES)
        if r:
            sh_s[r - 1, 0:n, :] = ext_s[r:r + n, :]
        for j in taps:
            a = (lo + j) // SUBLANES
            src = sh_s[r - 1, SUBLANES * a:SUBLANES * a + L, :] if r else ext_s[SUBLANES * a:SUBLANES * a + L, :]
            acc = acc + src * dww_ref[j:j + 1, :]
    mu = jnp.mean(acc, axis=-1, keepdims=True)
    xc = acc - mu
    var = jnp.mean(xc * xc, axis=-1, keepdims=True)
    y = xc * lax.rsqrt(var + EPS) * lng_ref[...] + lnb_ref[...]
    return _silu(y)


def _hgrn_chunk(q_s, k_s, lf_s, v_s, g_s, ng_ref, r0, L, S_s, c_s, oB_s):
    nb = L // SUB
    rows = pl.ds(r0, L)
    b_s, kc_s = c_s.at[0], c_s.at[1]
    b_s[...] = lf_s[rows, :]
    kc_s[...] = k_s[rows, :]
    trow = lax.broadcasted_iota(jnp.int32, (SUB, DKB), 0)
    colidx = lax.broadcasted_iota(jnp.int32, (SUB, L), 1)
    colblk = lax.shift_right_logical(colidx, SUB.bit_length() - 1)
    for h in range(HB):
        ls = slice(h * DKB, (h + 1) * DKB)
        b = b_s[:, ls]
        q = q_s[rows, ls]
        k = kc_s[:, ls]
        v = v_s[rows, ls]
        st = S_s[h]

        def row(r):
            return jnp.broadcast_to(b_s[r:r + 1, ls], (SUB, DKB))

        first = [row(SUB * i) for i in range(nb)]
        nxt = [row(SUB * (j + 1)) for j in range(nb - 1)] + [row(L - 1)]
        r_first = jnp.concatenate(first, axis=0)
        r_next = jnp.concatenate(nxt, axis=0)
        qt = q * jnp.exp2(b - r_first)
        kh = k * jnp.exp2(r_next - b)
        o = _dot_nt((qt * jnp.exp2(r_first)).astype(BF16), st.astype(BF16))
        lhs = []
        for i in range(1, nb):
            qi = qt[i * SUB:(i + 1) * SUB]
            for j in range(i):
                lhs.append(qi if j == i - 1 else qi * jnp.exp2(first[i] - nxt[j]))
        pair = _dot_nt(jnp.concatenate(lhs, axis=0).astype(BF16), kh.astype(BF16))
        a_rows = []
        p = 0
        for i in range(nb):
            acc = jnp.zeros((SUB, L), F32)
            for j in range(i):
                acc = jnp.where(colblk == j, pair[p * SUB:(p + 1) * SUB], acc)
                p += 1
            qb = q[i * SUB:(i + 1) * SUB]
            bb = b[i * SUB:(i + 1) * SUB]
            for s in range(SUB):
                g = i * SUB + s
                e = jnp.exp2(jnp.where(trow >= s, bb - b_s[g:g + 1, ls], NEG))
                w = jnp.sum(qb * kc_s[g:g + 1, ls] * e, axis=1, keepdims=True)
                acc = jnp.where(colidx == g, w, acc)
            a_rows.append(acc)
        a = jnp.concatenate(a_rows, axis=0)
        o = o + _dot(a.astype(BF16), v.astype(BF16))
        bl = b_s[L - 1:L, ls]
        k_end = (kh * jnp.exp2(bl - r_next)).astype(BF16)
        S_s[h] = st * jnp.exp2(bl) + _dot(v.T.astype(BF16), k_end)
        on = o * lax.rsqrt(jnp.mean(o * o, axis=-1, keepdims=True) + EPS)
        oB_s[rows, ls] = (on * ng_ref[:, ls] * g_s[rows, ls]).astype(BF16)


def _attn_chunk(q, kband, vband, bias_s, valid):
    L = q.shape[0]
    low = lax.broadcasted_iota(jnp.int32, (1, 2 * DHC), 1) < DHC
    outs = []
    for p in range(HC // 2):
        ls = slice(p * 2 * DHC, (p + 1) * 2 * DHC)
        qp = q[:, ls]
        q2 = jnp.concatenate([jnp.where(low, qp, 0.0), jnp.where(low, 0.0, qp)], axis=0).astype(BF16)
        s = _dot_nt(q2, kband[:, ls]) + bias_s[p]
        if valid is not None:
            s = jnp.where(valid, s, NEG)
        e = jnp.exp2(s - jnp.max(s, axis=-1, keepdims=True))
        den = jnp.sum(e, axis=-1, keepdims=True)
        pv = _dot(e.astype(BF16), vband[:, ls]) / den
        outs.append(jnp.where(low, pv[0:L], pv[L:2 * L]))
    return jnp.concatenate(outs, axis=1)


def _gate_block(i, xn_s, wg_ref, gate_s):
    xn = xn_s[...]
    for k in range(3):
        gate_s[k * N_MRG_BLK + i] = _sig(_dot(xn, wg_ref[k * N_MRG_BLK + i]))


def _merge_out(x, gate_s, branches, m_s, wm_ref):
    for i in range(N_MRG_BLK):
        cs = slice(i * MRG_BLK, (i + 1) * MRG_BLK)
        m = None
        for k, (src, w_ref) in enumerate(branches):
            term = gate_s[k * N_MRG_BLK + i] * _dot(src[...], w_ref[:, cs])
            m = term if m is None else m + term
        m_s[:, cs] = m.astype(BF16)
    return x + _dot(m_s[...], wm_ref[...])


def _mixer_prompt_kernel(layer, n_tiles,
                         x_ref, win_ref, wg_ref, gmix_ref, dww_ref, dwb_ref, lng_ref, lnb_ref, wc_ref,
                         lbl_ref, ng_ref, wh_ref, gtab_ref, wa_ref, wm_ref,
                         h_ref, nconv_ref, ns_ref, nk_ref, nv_ref,
                         xn_s, u_s, q_s, k_s, lf_s, v_s, g_s, aq_s, kb_s, vb_s, ext_s, sh_s, S_s, c_s,
                         hA_s, oB_s, oC_s, m_s, gate_s, bias_s):
    t = pl.program_id(1)

    @pl.when(t == 0)
    def _():
        ext_s[0:CONV_HIST, :] = jnp.zeros((CONV_HIST, WA), F32)
        S_s[...] = jnp.zeros_like(S_s)
        kb_s[...] = jnp.zeros_like(kb_s)
        vb_s[...] = jnp.zeros_like(vb_s)
        _build_bias(gtab_ref, bias_s, CHUNK, BAND_PAST + CHUNK)

    @pl.when(t > 0)
    def _():
        kb_s[0:TILE, :] = kb_s[TILE:2 * TILE, :]
        vb_s[0:TILE, :] = vb_s[TILE:2 * TILE, :]

    lb = _lower_bound(lbl_ref, layer)
    ck, cv = _project_in(x_ref[0], CHUNK, win_ref, gmix_ref, lb, xn_s, u_s, q_s, k_s, lf_s, v_s, g_s, aq_s)
    kb_s[TILE:2 * TILE, :] = ck.astype(BF16)
    vb_s[TILE:2 * TILE, :] = cv.astype(BF16)

    @pl.when(t == n_tiles - 1)
    def _():
        nk_ref[0] = ck
        nv_ref[0] = cv

    col = lax.broadcasted_iota(jnp.int32, (1, KEYS), 1)
    chunks_per_step = TILE // CHUNK // N_MRG_BLK

    def step(i, carry):
        _gate_block(i, xn_s, wg_ref, gate_s)
        for cc in range(chunks_per_step):
            r0 = pl.multiple_of((i * chunks_per_step + cc) * CHUNK, CHUNK)
            rows = pl.ds(r0, CHUNK)
            ext_s[CONV_HIST:CONV_HIST + CHUNK, :] = u_s[rows, :]
            hA_s[rows, :] = _conv_chunk(ext_s, sh_s, dww_ref, dwb_ref, lng_ref, lnb_ref, CHUNK).astype(BF16)
            ext_s[0:CONV_HIST, :] = ext_s[CHUNK:CHUNK + CONV_HIST, :]
            _hgrn_chunk(q_s, k_s, lf_s, v_s, g_s, ng_ref, r0, CHUNK, S_s, c_s, oB_s)
            valid = col >= BAND_PAST - r0 - t * TILE
            keys = pl.ds(r0, KEYS)
            oC_s[rows, :] = _attn_chunk(aq_s[rows, :], kb_s[keys, :], vb_s[keys, :], bias_s, valid).astype(BF16)
        return carry

    lax.fori_loop(0, N_MRG_BLK, step, 0)

    h_ref[0] = _merge_out(x_ref[0], gate_s, ((hA_s, wc_ref), (oB_s, wh_ref), (oC_s, wa_ref)), m_s, wm_ref)

    @pl.when(t == n_tiles - 1)
    def _():
        nconv_ref[0] = ext_s[CONV_HIST - (CONV_W - 1):CONV_HIST, :]
        for h in range(HB):
            ns_ref[0, h] = S_s[h].T


def _mixer_sample_kernel(layer, n_seq, L,
                         x_ref, conv_ref, s0_ref, cak_ref, cav_ref,
                         win_ref, wg_ref, gmix_ref, dww_ref, dwb_ref, lng_ref, lnb_ref, wc_ref,
                         lbl_ref, ng_ref, wh_ref, gtab_ref, wa_ref, wm_ref,
                         h_ref, nconv_ref, ns_ref, nk_ref, nv_ref,
                         xn_s, u_s, q_s, k_s, lf_s, v_s, g_s, aq_s, kb_s, vb_s, ext_s, sh_s, S_s, c_s,
                         hA_s, oB_s, oC_s, m_s, gate_s, bias_s, kn_s, vn_s):
    n = pl.program_id(0)
    n_cache = cak_ref.shape[1]

    @pl.when(n == 0)
    def _():
        lb = _lower_bound(lbl_ref, layer)
        ck, cv = _project_in(x_ref[...], L, win_ref, gmix_ref, lb, xn_s, u_s, q_s, k_s, lf_s, v_s, g_s, aq_s)
        nk_ref[...] = ck
        nv_ref[...] = cv
        kn_s[...] = ck.astype(BF16)
        vn_s[...] = cv.astype(BF16)
        kb_s[...] = jnp.zeros_like(kb_s)
        vb_s[...] = jnp.zeros_like(vb_s)
        _build_bias(gtab_ref, bias_s, L, n_cache + L)

    r0 = pl.multiple_of(n * L, L)
    rows = pl.ds(r0, L)
    ext_s[0:CONV_HIST, :] = conv_ref[0]
    ext_s[CONV_HIST:CONV_HIST + L, :] = u_s[rows, :]
    hA_s[rows, :] = _conv_chunk(ext_s, sh_s, dww_ref, dwb_ref, lng_ref, lnb_ref, L).astype(BF16)
    nconv_ref[0] = ext_s[L + CONV_HIST - (CONV_W - 1):L + CONV_HIST, :]
    for h in range(HB):
        S_s[h] = s0_ref[0, h].T
    _hgrn_chunk(q_s, k_s, lf_s, v_s, g_s, ng_ref, r0, L, S_s, c_s, oB_s)
    for h in range(HB):
        ns_ref[0, h] = S_s[h].T
    kb_s[0:n_cache, :] = cak_ref[0].astype(BF16)
    vb_s[0:n_cache, :] = cav_ref[0].astype(BF16)
    kb_s[n_cache:n_cache + L, :] = kn_s[rows, :]
    vb_s[n_cache:n_cache + L, :] = vn_s[rows, :]
    oC_s[rows, :] = _attn_chunk(aq_s[rows, :], kb_s[...], vb_s[...], bias_s, None).astype(BF16)

    @pl.when(n == n_seq - 1)
    def _():
        for i in range(N_MRG_BLK):
            _gate_block(i, xn_s, wg_ref, gate_s)
        h_ref[...] = _merge_out(x_ref[...], gate_s, ((hA_s, wc_ref), (oB_s, wh_ref), (oC_s, wa_ref)), m_s, wm_ref)


def _ffn_kernel(final, n_tiles, n_seq,
                h_ref, st_ref, gffn_ref, wu_ref, dw_ref, wd_ref, gfin_ref,
                out_ref, nffn_ref,
                hn_s, acc_s, tail_s, ext_s):
    t = pl.program_id(1)
    rows_t = h_ref.shape[1]
    keep = 2 * n_seq
    hist = tail_s.shape[0]

    @pl.when(t == 0)
    def _():
        tail_s[...] = jnp.zeros_like(tail_s)
        tail_s[hist - keep:hist, :] = st_ref[0]

    h = h_ref[0]
    hn_s[...] = _rms(h, gffn_ref[...]).astype(BF16)
    acc_s[...] = jnp.zeros_like(acc_s)

    def cols(ab, j):
        return slice(ab * D_FF + j * FF_BLK, ab * D_FF + (j + 1) * FF_BLK)

    def up(j, slot):
        hn = hn_s[...]
        for ab in range(2):
            u = _dot(hn, wu_ref[:, cols(ab, j)])
            ext_s[slot, ab, 0:hist, :] = tail_s[:, cols(ab, j)]
            ext_s[slot, ab, hist:hist + rows_t, :] = u
            tail_s[:, cols(ab, j)] = u[rows_t - hist:rows_t]

    def conv3(slot, ab, j):
        w = dw_ref[:, cols(ab, j)]
        u = ext_s[slot, ab, hist:hist + rows_t, :]
        u1 = ext_s[slot, ab, hist - n_seq:hist - n_seq + rows_t, :]
        u2 = ext_s[slot, ab, hist - keep:hist - keep + rows_t, :]
        return u * w[2:3] + u1 * w[1:2] + u2 * w[0:1]

    def down(j, slot):
        g = (_silu(conv3(slot, 0, j)) * conv3(slot, 1, j)).astype(BF16)
        acc_s[...] += _dot(g, wd_ref[j * FF_BLK:(j + 1) * FF_BLK, :])

    up(0, 0)
    for j in range(N_FF_BLK):
        if j + 1 < N_FF_BLK:
            up(j + 1, (j + 1) % 2)
        down(j, j % 2)
    y = h + acc_s[...]
    if final:
        y = _rms(y, gfin_ref[...])
    out_ref[0] = y

    @pl.when(t == n_tiles - 1)
    def _():
        nffn_ref[0] = tail_s[hist - keep:hist, :]


def _const_spec(shape):
    zeros = (0,) * len(shape)
    return pl.BlockSpec(shape, lambda *_: zeros, pipeline_mode=pl.Buffered(1))


def _bias_vector(tab, n_q, n_k):
    span = n_q + n_k - 1
    n_flat = BAND_PAST + n_q - 1 - REL_CLIP
    lo = BAND_PAST + n_q - 1 - (span - 1) + REL_CLIP
    assert lo >= 0 and n_flat >= 0 and span <= BIAS_SPAN
    g = jnp.concatenate([jnp.broadcast_to(tab[:, 2 * REL_CLIP:], (HC, n_flat)),
                         tab[:, lo:2 * REL_CLIP + 1][:, ::-1],
                         jnp.zeros((HC, BIAS_SPAN - span), F32)], axis=1)
    return g * LOG2E


def _mixer_scratch(rows, L, kb_rows):
    return [
        pltpu.VMEM((rows, D_MODEL), BF16),
        pltpu.VMEM((rows, WA), F32),
        pltpu.VMEM((rows, WB), F32),
        pltpu.VMEM((rows, WB), F32),
        pltpu.VMEM((rows, WB), F32),
        pltpu.VMEM((rows, WB), F32),
        pltpu.VMEM((rows, WB), F32),
        pltpu.VMEM((rows, WC), F32),
        pltpu.VMEM((kb_rows, WC), BF16),
        pltpu.VMEM((kb_rows, WC), BF16),
        pltpu.VMEM((CONV_HIST + L, WA), F32),
        pltpu.VMEM((SUBLANES - 1, CONV_HIST - SUBLANES + L, WA), F32),
        pltpu.VMEM((HB, DVB, DKB), F32),
        pltpu.VMEM((2, L, WB), F32),
        pltpu.VMEM((rows, WA), BF16),
        pltpu.VMEM((rows, WB), BF16),
        pltpu.VMEM((rows, WC), BF16),
        pltpu.VMEM((rows, D_MODEL), BF16),
        pltpu.VMEM((3 * N_MRG_BLK, rows, MRG_BLK), F32),
        pltpu.VMEM((HC // 2, 2 * L, KEYS), F32),
    ]


def _layer_weights(l, w_in, conv_dw_w, conv_dw_b, conv_ln_g, conv_ln_b, w_conv_out, hgrn_lb_logits,
                   hgrn_norm_g, w_hgrn_out, w_attn_out, w_mix_out, g_mix):
    wg = w_in[l][:, C_GATE:].astype(BF16).reshape(D_MODEL, 3 * N_MRG_BLK, MRG_BLK).transpose(1, 0, 2)
    return dict(
        win=w_in[l][:, :C_GATE].astype(BF16), wg=wg,
        gmix=g_mix[l][None, :], dww=conv_dw_w[l], dwb=conv_dw_b[l][None, :],
        lng=conv_ln_g[l][None, :], lnb=conv_ln_b[l][None, :], wc=w_conv_out[l].astype(BF16),
        lbl=hgrn_lb_logits, ng=hgrn_norm_g[l][None, :], wh=w_hgrn_out[l].astype(BF16),
        wa=w_attn_out[l].astype(BF16), wm=w_mix_out[l].astype(BF16))


def _weight_specs(w, gtab):
    names = ("win", "wg", "gmix", "dww", "dwb", "lng", "lnb", "wc", "lbl", "ng", "wh")
    args = [w[k] for k in names] + [gtab, w["wa"], w["wm"]]
    return args, [_const_spec(a.shape) for a in args]


def _mixer_prompt(layer, x, w, gtab):
    n, t_len, _ = x.shape
    n_tiles = t_len // TILE
    wargs, wspecs = _weight_specs(w, gtab)
    out_shape = (
        jax.ShapeDtypeStruct((n, t_len, D_MODEL), F32),
        jax.ShapeDtypeStruct((n, CONV_W - 1, WA), F32),
        jax.ShapeDtypeStruct((n, HB, DKB, DVB), F32),
        jax.ShapeDtypeStruct((n, BAND_PAST, WC), F32),
        jax.ShapeDtypeStruct((n, BAND_PAST, WC), F32),
    )
    out_specs = (
        pl.BlockSpec((1, TILE, D_MODEL), lambda i, t: (i, t, 0)),
        pl.BlockSpec((1, CONV_W - 1, WA), lambda i, t: (i, 0, 0)),
        pl.BlockSpec((1, HB, DKB, DVB), lambda i, t: (i, 0, 0, 0)),
        pl.BlockSpec((1, BAND_PAST, WC), lambda i, t: (i, 0, 0)),
        pl.BlockSpec((1, BAND_PAST, WC), lambda i, t: (i, 0, 0)),
    )
    return pl.pallas_call(
        functools.partial(_mixer_prompt_kernel, layer, n_tiles),
        grid=(n, n_tiles),
        in_specs=[pl.BlockSpec((1, TILE, D_MODEL), lambda i, t: (i, t, 0))] + wspecs,
        out_specs=out_specs,
        out_shape=out_shape,
        scratch_shapes=_mixer_scratch(TILE, CHUNK, TILE - CHUNK + KEYS),
        compiler_params=pltpu.CompilerParams(dimension_semantics=("arbitrary", "arbitrary"),
                                             vmem_limit_bytes=VMEM_LIMIT),
        name=f"mixer_prompt_l{layer}",
    )(x, *wargs)


def _mixer_sample(layer, x, conv_state, s0, cache_k, cache_v, w, gtab):
    n, L, _ = x.shape
    rows = n * L
    n_cache = cache_k.shape[1]
    wargs, wspecs = _weight_specs(w, gtab)
    conv_pad = jnp.pad(conv_state, ((0, 0), (CONV_HIST - (CONV_W - 1), 0), (0, 0)))
    out_shape = (
        jax.ShapeDtypeStruct((rows, D_MODEL), F32),
        jax.ShapeDtypeStruct((n, CONV_W - 1, WA), F32),
        jax.ShapeDtypeStruct((n, HB, DKB, DVB), F32),
        jax.ShapeDtypeStruct((rows, WC), F32),
        jax.ShapeDtypeStruct((rows, WC), F32),
    )
    out_specs = (
        pl.BlockSpec((rows, D_MODEL), lambda i: (0, 0)),
        pl.BlockSpec((1, CONV_W - 1, WA), lambda i: (i, 0, 0)),
        pl.BlockSpec((1, HB, DKB, DVB), lambda i: (i, 0, 0, 0)),
        pl.BlockSpec((rows, WC), lambda i: (0, 0)),
        pl.BlockSpec((rows, WC), lambda i: (0, 0)),
    )
    in_specs = [
        pl.BlockSpec((rows, D_MODEL), lambda i: (0, 0)),
        pl.BlockSpec((1, CONV_HIST, WA), lambda i: (i, 0, 0)),
        pl.BlockSpec((1, HB, DKB, DVB), lambda i: (i, 0, 0, 0)),
        pl.BlockSpec((1, n_cache, WC), lambda i: (i, 0, 0)),
        pl.BlockSpec((1, n_cache, WC), lambda i: (i, 0, 0)),
    ] + wspecs
    return pl.pallas_call(
        functools.partial(_mixer_sample_kernel, layer, n, L),
        grid=(n,),
        in_specs=in_specs,
        out_specs=out_specs,
        out_shape=out_shape,
        scratch_shapes=_mixer_scratch(rows, L, KEYS) + [pltpu.VMEM((rows, WC), BF16), pltpu.VMEM((rows, WC), BF16)],
        compiler_params=pltpu.CompilerParams(dimension_semantics=("arbitrary",),
                                             vmem_limit_bytes=VMEM_LIMIT),
        name=f"mixer_sample_l{layer}",
    )(x.reshape(rows, D_MODEL), conv_pad, s0, cache_k, cache_v, *wargs)


def _ffn(layer, final, h, state, n_seq, w_ffn_up, ffn_dw_w, w_ffn_down, g_ffn, g_final):
    groups, rows, _ = h.shape
    rows_t = min(rows, TILE)
    n_tiles = rows // rows_t
    keep = 2 * n_seq
    hist = max(SUBLANES, keep)
    args = [h, state, g_ffn[layer][None, :], w_ffn_up[layer].astype(BF16), ffn_dw_w[layer],
            w_ffn_down[layer].astype(BF16), g_final[None, :]]
    in_specs = [
        pl.BlockSpec((1, rows_t, D_MODEL), lambda i, t: (i, t, 0)),
        pl.BlockSpec((1, keep, 2 * D_FF), lambda i, t: (i, 0, 0)),
    ] + [_const_spec(a.shape) for a in args[2:]]
    return pl.pallas_call(
        functools.partial(_ffn_kernel, final, n_tiles, n_seq),
        grid=(groups, n_tiles),
        in_specs=in_specs,
        out_specs=(pl.BlockSpec((1, rows_t, D_MODEL), lambda i, t: (i, t, 0)),
                   pl.BlockSpec((1, keep, 2 * D_FF), lambda i, t: (i, 0, 0))),
        out_shape=(jax.ShapeDtypeStruct((groups, rows, D_MODEL), F32),
                   jax.ShapeDtypeStruct((groups, keep, 2 * D_FF), F32)),
        scratch_shapes=[pltpu.VMEM((rows_t, D_MODEL), BF16),
                        pltpu.VMEM((rows_t, D_MODEL), F32),
                        pltpu.VMEM((hist, 2 * D_FF), F32),
                        pltpu.VMEM((2, 2, hist + rows_t, FF_BLK), F32)],
        compiler_params=pltpu.CompilerParams(dimension_semantics=("arbitrary", "arbitrary"),
                                             vmem_limit_bytes=VMEM_LIMIT),
        name=f"ffn_l{layer}_{'s' if n_seq > 1 else 'p'}",
    )(*args)


def kernel(x_prompt, x_sample, state_conv, state_hgrn, cache_attn_k, cache_attn_v, state_ffn, w_in, conv_dw_w, conv_dw_b, conv_ln_g, conv_ln_b, w_conv_out, hgrn_lb_logits, hgrn_norm_g, w_hgrn_out, attn_rel_bias, w_attn_out, w_mix_out, g_mix, w_ffn_up, ffn_dw_w, w_ffn_down, g_ffn, g_final):
    depth = w_in.shape[0]
    n_p, t_p, _ = x_prompt.shape
    n_s, t_s, _ = x_sample.shape
    n_cache = cache_attn_k.shape[2]
    assert t_p % TILE == 0 and TILE == BAND_PAST and n_cache == min(BAND_PAST, PAST_LEN)
    assert n_cache + t_s <= KEYS and t_s % SUB == 0 and t_s >= CONV_W - 1

    xp, xs = x_prompt, x_sample
    outs = [[] for _ in range(10)]
    for l in range(depth):
        w = _layer_weights(l, w_in, conv_dw_w, conv_dw_b, conv_ln_g, conv_ln_b, w_conv_out, hgrn_lb_logits,
                           hgrn_norm_g, w_hgrn_out, w_attn_out, w_mix_out, g_mix)
        final = l == depth - 1
        hp, c1, s1, k1, v1 = _mixer_prompt(l, xp, w, _bias_vector(attn_rel_bias[l], CHUNK, BAND_PAST + CHUNK))
        zero_ffn = jnp.zeros((n_p, FFN_CONV_W - 1, 2 * D_FF), F32)
        xp, f1 = _ffn(l, final, hp, zero_ffn, 1, w_ffn_up, ffn_dw_w, w_ffn_down, g_ffn, g_final)
        hs, c2, s2, k2, v2 = _mixer_sample(
            l, xs, state_conv[l], state_hgrn[l], cache_attn_k[l].reshape(n_s, n_cache, WC),
            cache_attn_v[l].reshape(n_s, n_cache, WC), w, _bias_vector(attn_rel_bias[l], t_s, n_cache + t_s))
        hs_tm = hs.reshape(n_s, t_s, D_MODEL).transpose(1, 0, 2).reshape(1, t_s * n_s, D_MODEL)
        st_tm = state_ffn[l].transpose(1, 0, 2).reshape(1, (FFN_CONV_W - 1) * n_s, 2 * D_FF)
        ys_tm, f2_tm = _ffn(l, final, hs_tm, st_tm, n_s, w_ffn_up, ffn_dw_w, w_ffn_down, g_ffn, g_final)
        xs = ys_tm.reshape(t_s, n_s, D_MODEL).transpose(1, 0, 2)
        f2 = f2_tm.reshape(FFN_CONV_W - 1, n_s, 2 * D_FF).transpose(1, 0, 2)
        for lst, val in zip(outs, (c1, c2, s1, s2,
                                   k1.reshape(n_p, BAND_PAST, HC, DHC), v1.reshape(n_p, BAND_PAST, HC, DHC),
                                   k2.reshape(n_s, t_s, HC, DHC), v2.reshape(n_s, t_s, HC, DHC), f1, f2)):
            lst.append(val)
    return (xp, xs) + tuple(jnp.stack(o) for o in outs)
```

```python
import functools
import math

import jax
import jax.numpy as jnp
from jax import lax
from jax.experimental import pallas as pl
from jax.experimental.pallas import tpu as pltpu

F32 = jnp.float32
BF16 = jnp.bfloat16

D_MODEL = 1024
CHUNK = 64
WA = 256
CONV_W = 31
HB = 4
DKB = 128
DVB = 128
WB = HB * DVB
HC = 4
DHC = 64
WC = HC * DHC
BAND_CHUNKS = 8
BAND_PAST = BAND_CHUNKS * CHUNK
REL_CLIP = 128
D_FF = 2816
FFN_CONV_W = 3
EPS = 1e-6
PAST_LEN = 2048

C_A = 0
C_BQ = 2 * WA
C_BF = C_BQ + HB * DKB
C_BI = C_BF + HB * DKB
C_BG = C_BI + WB
C_C = C_BG + WB
C_GATE = C_C + 3 * WC
N_IN = C_GATE + 3 * D_MODEL

TILE = 512
SUB = 8
KEYS = 768
BIAS_SPAN = 768
CONV_HIST = 32
FF_BLK = 256
N_FF_BLK = D_FF // FF_BLK
MRG_BLK = 512
N_MRG_BLK = D_MODEL // MRG_BLK
NEG = -1e30
LOG2E = math.log2(math.e)
SUBLANES = 8
V7X_VMEM_BYTES = 64 * 1024 * 1024
VMEM_LIMIT = V7X_VMEM_BYTES - 8 * 1024 * 1024


def _dot(a, b):
    return jnp.dot(a, b, preferred_element_type=F32)


def _dot_nt(a, b):
    return lax.dot_general(a, b, (((1,), (1,)), ((), ())), preferred_element_type=F32)


def _sig(x):
    return jax.nn.sigmoid(x)


def _silu(x):
    return x * jax.nn.sigmoid(x)


def _rms(x, g):
    return x * lax.rsqrt(jnp.mean(x * x, axis=-1, keepdims=True) + EPS) * g


def _lower_bound(lbl_ref, layer):
    lg = lbl_ref[...]
    e = jnp.exp(lg - jnp.max(lg, axis=0, keepdims=True))
    p = e / jnp.sum(e, axis=0, keepdims=True)
    return jnp.sum(p[0:layer + 1], axis=0, keepdims=True) - p[0:1]


def _cumsum_blocks(x, L):
    rowmod = lax.broadcasted_iota(jnp.int32, x.shape, 0) & (L - 1)
    step = 1
    while step < L:
        x = x + jnp.where(rowmod >= step, pltpu.roll(x, step, axis=0), 0.0)
        step *= 2
    return x


def _project_in(x, L, win_ref, gmix_ref, lb, xn_s, u_s, q_s, k_s, lf_s, v_s, g_s, aq_s):
    xn = _rms(x, gmix_ref[...]).astype(BF16)
    xn_s[...] = xn
    a = _dot(xn, win_ref[:, C_A:C_A + 2 * WA])
    u_s[...] = a[:, :WA] * _sig(a[:, WA:])
    q_s[...] = _silu(_dot(xn, win_ref[:, C_BQ:C_BF]))
    zf = _dot(xn, win_ref[:, C_BF:C_BI])
    lf_s[...] = _cumsum_blocks(jnp.log(lb + (1.0 - lb) * _sig(zf)) * LOG2E, L)
    k_s[...] = (1.0 - lb) * _sig(-zf)
    v_s[...] = _dot(xn, win_ref[:, C_BI:C_BG])
    g_s[...] = _silu(_dot(xn, win_ref[:, C_BG:C_C]))
    zc = _dot(xn, win_ref[:, C_C:C_GATE])
    aq_s[...] = zc[:, :WC] * (DHC ** -0.5 * LOG2E)
    return zc[:, WC:2 * WC], zc[:, 2 * WC:]


def _build_bias(gtab_ref, bias_s, L, n_k, n_chunks):
    col = lax.broadcasted_iota(jnp.int32, (L, KEYS), 1)
    for h in range(HC):
        x = jnp.broadcast_to(gtab_ref[h:h + 1, :], (L, BIAS_SPAN))
        for cc in range(n_chunks):
            y = pltpu.roll(x, (BIAS_SPAN - (L - 1) + cc * L) % BIAS_SPAN, axis=1, stride=1, stride_axis=0)
            r0 = (cc * 2 + h % 2) * L
            inside = (col >= cc * L) & (col < cc * L + n_k)
            bias_s[h // 2, r0:r0 + L, :] = jnp.where(inside, y[:, :KEYS], NEG)


def _conv_chunk(ext_s, sh_s, dww_ref, dwb_ref, lng_ref, lnb_ref, L):
    lo = CONV_HIST - (CONV_W - 1)
    acc = jnp.zeros((L, WA), F32) + dwb_ref[...]
    for r in range(SUBLANES):
        taps = [p - lo for p in range(r, lo + CONV_W, SUBLANES) if p >= lo]
        n = L + SUBLANES * ((lo + taps[-1]) // SUBLANES)
        if r:
            sh_s[r - 1, 0:n, :] = ext_s[r:r + n, :]
        for j in taps:
            a = (lo + j) // SUBLANES
            src = sh_s[r - 1, SUBLANES * a:SUBLANES * a + L, :] if r else ext_s[SUBLANES * a:SUBLANES * a + L, :]
            acc = acc + src * dww_ref[j:j + 1, :]
    mu = jnp.mean(acc, axis=-1, keepdims=True)
    xc = acc - mu
    var = jnp.mean(xc * xc, axis=-1, keepdims=True)
    y = xc * lax.rsqrt(var + EPS) * lng_ref[...] + lnb_ref[...]
    return _silu(y)


def _hgrn_inblock(q_s, k_s, lf_s, r0, L, b_s, kc_s, wd_s):
    nb = L // SUB
    rows = pl.ds(r0, L)
    b_s[...] = lf_s[rows, :]
    kc_s[...] = k_s[rows, :]
    trow = lax.broadcasted_iota(jnp.int32, (SUB, DKB), 0)
    colidx = lax.broadcasted_iota(jnp.int32, (SUB, L), 1)
    for h in range(HB):
        ls = slice(h * DKB, (h + 1) * DKB)
        for i in range(nb):
            qb = q_s[pl.ds(r0 + i * SUB, SUB), ls]
            bb = b_s[i * SUB:(i + 1) * SUB, ls]
            acc = jnp.zeros((SUB, L), F32)
            for s in range(SUB):
                g = i * SUB + s
                e = jnp.exp2(jnp.where(trow >= s, bb - b_s[g:g + 1, ls], NEG))
                w = jnp.sum(qb * kc_s[g:g + 1, ls] * e, axis=1, keepdims=True)
                acc = jnp.where(colidx == g, w, acc)
            wd_s[h, i * SUB:(i + 1) * SUB, :] = acc


def _hgrn_chunk(q_s, k_s, lf_s, v_s, g_s, ng_ref, r0, L, S_s, b_s, oB_s, wd_s):
    nb = L // SUB
    rows = pl.ds(r0, L)
    b_s[...] = lf_s[rows, :]
    colidx = lax.broadcasted_iota(jnp.int32, (SUB, L), 1)
    colblk = lax.shift_right_logical(colidx, SUB.bit_length() - 1)
    for h in range(HB):
        ls = slice(h * DKB, (h + 1) * DKB)
        b = b_s[:, ls]
        q = q_s[rows, ls]
        k = k_s[rows, ls]
        v = v_s[rows, ls]
        st = S_s[h]

        def row(r):
            return jnp.broadcast_to(b_s[r:r + 1, ls], (SUB, DKB))

        first = [row(SUB * i) for i in range(nb)]
        nxt = [row(SUB * (j + 1)) for j in range(nb - 1)] + [row(L - 1)]
        r_first = jnp.concatenate(first, axis=0)
        r_next = jnp.concatenate(nxt, axis=0)
        qt = q * jnp.exp2(b - r_first)
        kh = k * jnp.exp2(r_next - b)
        o = _dot_nt((qt * jnp.exp2(r_first)).astype(BF16), st.astype(BF16))
        lhs = []
        for i in range(1, nb):
            qi = qt[i * SUB:(i + 1) * SUB]
            for j in range(i):
                lhs.append(qi if j == i - 1 else qi * jnp.exp2(first[i] - nxt[j]))
        pair = _dot_nt(jnp.concatenate(lhs, axis=0).astype(BF16), kh.astype(BF16))
        a_rows = []
        p = 0
        for i in range(nb):
            acc = jnp.zeros((SUB, L), F32)
            for j in range(i):
                acc = jnp.where(colblk == j, pair[p * SUB:(p + 1) * SUB], acc)
                p += 1
            a_rows.append(acc)
        a = jnp.concatenate(a_rows, axis=0) + wd_s[h]
        o = o + _dot(a.astype(BF16), v.astype(BF16))
        bl = b_s[L - 1:L, ls]
        k_end = (kh * jnp.exp2(bl - r_next)).astype(BF16)
        S_s[h] = st * jnp.exp2(bl) + _dot(v.T.astype(BF16), k_end)
        on = o * lax.rsqrt(jnp.mean(o * o, axis=-1, keepdims=True) + EPS)
        oB_s[rows, ls] = (on * ng_ref[:, ls] * g_s[rows, ls]).astype(BF16)


def _attn_chunk(qs, kband, vband, bias_s, valid):
    L = qs[0].shape[0]
    low = lax.broadcasted_iota(jnp.int32, (1, 2 * DHC), 1) < DHC
    outs = [[] for _ in qs]
    for p in range(HC // 2):
        ls = slice(p * 2 * DHC, (p + 1) * 2 * DHC)
        stack = []
        for q in qs:
            stack += [jnp.where(low, q[:, ls], 0.0), jnp.where(low, 0.0, q[:, ls])]
        s = _dot_nt(jnp.concatenate(stack, axis=0).astype(BF16), kband[:, ls]) + bias_s[p]
        if valid is not None:
            s = jnp.where(valid, s, NEG)
        e = jnp.exp2(s - jnp.max(s, axis=-1, keepdims=True))
        den = jnp.sum(e, axis=-1, keepdims=True)
        pv = _dot(e.astype(BF16), vband[:, ls]) / den
        for cc in range(len(qs)):
            outs[cc].append(jnp.where(low, pv[2 * cc * L:(2 * cc + 1) * L], pv[(2 * cc + 1) * L:(2 * cc + 2) * L]))
    return [jnp.concatenate(o, axis=1) for o in outs]


def _gate_block(i, xn_s, wg_ref, gate_s):
    xn = xn_s[...]
    for k in range(3):
        gate_s[k * N_MRG_BLK + i] = _sig(_dot(xn, wg_ref[k * N_MRG_BLK + i]))


def _merge_out(x, gate_s, branches, m_s, wm_ref):
    for i in range(N_MRG_BLK):
        cs = slice(i * MRG_BLK, (i + 1) * MRG_BLK)
        m = None
        for k, (src, w_ref) in enumerate(branches):
            term = gate_s[k * N_MRG_BLK + i] * _dot(src[...], w_ref[:, cs])
            m = term if m is None else m + term
        m_s[:, cs] = m.astype(BF16)
    return x + _dot(m_s[...], wm_ref[...])


def _mixer_prompt_kernel(layer, n_tiles,
                         x_ref, win_ref, wg_ref, gmix_ref, dww_ref, dwb_ref, lng_ref, lnb_ref, wc_ref,
                         lbl_ref, ng_ref, wh_ref, gtab_ref, wa_ref, wm_ref,
                         h_ref, nconv_ref, ns_ref, nk_ref, nv_ref,
                         xn_s, u_s, q_s, k_s, lf_s, v_s, g_s, aq_s, kb_s, vb_s, ext_s, sh_s, S_s, c_s,
                         hA_s, oB_s, oC_s, m_s, gate_s, bias_s, wd_s, cb_s):
    t = pl.program_id(1)

    @pl.when(t == 0)
    def _():
        ext_s[0:CONV_HIST, :] = jnp.zeros((CONV_HIST, WA), F32)
        S_s[...] = jnp.zeros_like(S_s)
        kb_s[...] = jnp.zeros_like(kb_s)
        vb_s[...] = jnp.zeros_like(vb_s)
        _build_bias(gtab_ref, bias_s, CHUNK, BAND_PAST + CHUNK, TILE // CHUNK // N_MRG_BLK)

    @pl.when(t > 0)
    def _():
        kb_s[0:TILE, :] = kb_s[TILE:2 * TILE, :]
        vb_s[0:TILE, :] = vb_s[TILE:2 * TILE, :]

    lb = _lower_bound(lbl_ref, layer)
    ck, cv = _project_in(x_ref[0], CHUNK, win_ref, gmix_ref, lb, xn_s, u_s, q_s, k_s, lf_s, v_s, g_s, aq_s)
    kb_s[TILE:2 * TILE, :] = ck.astype(BF16)
    vb_s[TILE:2 * TILE, :] = cv.astype(BF16)

    @pl.when(t == n_tiles - 1)
    def _():
        nk_ref[0] = ck
        nv_ref[0] = cv

    col = lax.broadcasted_iota(jnp.int32, (1, KEYS), 1)
    chunks_per_step = TILE // CHUNK // N_MRG_BLK

    def prep(i):
        _gate_block(i, xn_s, wg_ref, gate_s)
        for cc in range(chunks_per_step):
            c = i * chunks_per_step + cc
            r0 = pl.multiple_of(c * CHUNK, CHUNK)
            rows = pl.ds(r0, CHUNK)
            ext_s[CONV_HIST:CONV_HIST + CHUNK, :] = u_s[rows, :]
            hA_s[rows, :] = _conv_chunk(ext_s, sh_s, dww_ref, dwb_ref, lng_ref, lnb_ref, CHUNK).astype(BF16)
            ext_s[0:CONV_HIST, :] = ext_s[CHUNK:CHUNK + CONV_HIST, :]
            _hgrn_inblock(q_s, k_s, lf_s, r0, CHUNK, c_s.at[2 * cc], c_s.at[2 * cc + 1], wd_s.at[c])

    def chain(i):
        for cc in range(chunks_per_step):
            c = i * chunks_per_step + cc
            r0 = pl.multiple_of(c * CHUNK, CHUNK)
            _hgrn_chunk(q_s, k_s, lf_s, v_s, g_s, ng_ref, r0, CHUNK, S_s, cb_s.at[cc], oB_s, wd_s.at[c])
        r0 = pl.multiple_of(i * chunks_per_step * CHUNK, CHUNK)
        valid = col >= BAND_PAST - r0 - t * TILE
        keys = pl.ds(r0, KEYS)
        qs = [aq_s[pl.ds(r0 + cc * CHUNK, CHUNK), :] for cc in range(chunks_per_step)]
        outs = _attn_chunk(qs, kb_s[keys, :], vb_s[keys, :], bias_s, valid)
        for cc in range(chunks_per_step):
            oC_s[pl.ds(r0 + cc * CHUNK, CHUNK), :] = outs[cc].astype(BF16)

    def step(i, carry):
        prep(i)
        chain(i)
        return carry

    lax.fori_loop(0, N_MRG_BLK, step, 0)

    h_ref[0] = _merge_out(x_ref[0], gate_s, ((hA_s, wc_ref), (oB_s, wh_ref), (oC_s, wa_ref)), m_s, wm_ref)

    @pl.when(t == n_tiles - 1)
    def _():
        nconv_ref[0] = ext_s[CONV_HIST - (CONV_W - 1):CONV_HIST, :]
        for h in range(HB):
            ns_ref[0, h] = S_s[h].T


def _mixer_sample_kernel(layer, n_seq, L,
                         x_ref, conv_ref, s0_ref, cak_ref, cav_ref,
                         win_ref, wg_ref, gmix_ref, dww_ref, dwb_ref, lng_ref, lnb_ref, wc_ref,
                         lbl_ref, ng_ref, wh_ref, gtab_ref, wa_ref, wm_ref,
                         h_ref, nconv_ref, ns_ref, nk_ref, nv_ref,
                         xn_s, u_s, q_s, k_s, lf_s, v_s, g_s, aq_s, kb_s, vb_s, ext_s, sh_s, S_s, c_s,
                         hA_s, oB_s, oC_s, m_s, gate_s, bias_s, wd_s, kn_s, vn_s):
    n = pl.program_id(0)
    n_cache = cak_ref.shape[1]

    @pl.when(n == 0)
    def _():
        lb = _lower_bound(lbl_ref, layer)
        ck, cv = _project_in(x_ref[...], L, win_ref, gmix_ref, lb, xn_s, u_s, q_s, k_s, lf_s, v_s, g_s, aq_s)
        nk_ref[...] = ck
        nv_ref[...] = cv
        kn_s[...] = ck.astype(BF16)
        vn_s[...] = cv.astype(BF16)
        kb_s[...] = jnp.zeros_like(kb_s)
        vb_s[...] = jnp.zeros_like(vb_s)
        _build_bias(gtab_ref, bias_s, L, n_cache + L, 1)

    r0 = pl.multiple_of(n * L, L)
    rows = pl.ds(r0, L)
    ext_s[0:CONV_HIST, :] = conv_ref[0]
    ext_s[CONV_HIST:CONV_HIST + L, :] = u_s[rows, :]
    hA_s[rows, :] = _conv_chunk(ext_s, sh_s, dww_ref, dwb_ref, lng_ref, lnb_ref, L).astype(BF16)
    nconv_ref[0] = ext_s[L + CONV_HIST - (CONV_W - 1):L + CONV_HIST, :]
    for h in range(HB):
        S_s[h] = s0_ref[0, h].T
    _hgrn_inblock(q_s, k_s, lf_s, r0, L, c_s.at[0], c_s.at[1], wd_s.at[0])
    _hgrn_chunk(q_s, k_s, lf_s, v_s, g_s, ng_ref, r0, L, S_s, c_s.at[0], oB_s, wd_s.at[0])
    for h in range(HB):
        ns_ref[0, h] = S_s[h].T
    kb_s[0:n_cache, :] = cak_ref[0].astype(BF16)
    vb_s[0:n_cache, :] = cav_ref[0].astype(BF16)
    kb_s[n_cache:n_cache + L, :] = kn_s[rows, :]
    vb_s[n_cache:n_cache + L, :] = vn_s[rows, :]
    oC_s[rows, :] = _attn_chunk([aq_s[rows, :]], kb_s[...], vb_s[...], bias_s, None)[0].astype(BF16)

    @pl.when(n == n_seq - 1)
    def _():
        for i in range(N_MRG_BLK):
            _gate_block(i, xn_s, wg_ref, gate_s)
        h_ref[...] = _merge_out(x_ref[...], gate_s, ((hA_s, wc_ref), (oB_s, wh_ref), (oC_s, wa_ref)), m_s, wm_ref)


def _ffn_kernel(final, n_tiles, n_seq,
                h_ref, st_ref, gffn_ref, wu_ref, dw_ref, wd_ref, gfin_ref,
                out_ref, nffn_ref,
                hn_s, acc_s, tail_s, ext_s):
    t = pl.program_id(1)
    rows_t = h_ref.shape[1]
    keep = 2 * n_seq
    hist = tail_s.shape[0]

    @pl.when(t == 0)
    def _():
        tail_s[...] = jnp.zeros_like(tail_s)
        tail_s[hist - keep:hist, :] = st_ref[0]

    h = h_ref[0]
    hn_s[...] = _rms(h, gffn_ref[...]).astype(BF16)
    acc_s[...] = jnp.zeros_like(acc_s)

    def cols(ab, j):
        return slice(ab * D_FF + j * FF_BLK, ab * D_FF + (j + 1) * FF_BLK)

    def up(j, slot):
        hn = hn_s[...]
        for ab in range(2):
            u = _dot(hn, wu_ref[:, cols(ab, j)])
            ext_s[slot, ab, 0:hist, :] = tail_s[:, cols(ab, j)]
            ext_s[slot, ab, hist:hist + rows_t, :] = u
            tail_s[:, cols(ab, j)] = u[rows_t - hist:rows_t]

    def conv3(slot, ab, j):
        w = dw_ref[:, cols(ab, j)]
        u = ext_s[slot, ab, hist:hist + rows_t, :]
        u1 = ext_s[slot, ab, hist - n_seq:hist - n_seq + rows_t, :]
        u2 = ext_s[slot, ab, hist - keep:hist - keep + rows_t, :]
        return u * w[2:3] + u1 * w[1:2] + u2 * w[0:1]

    def down(j, slot):
        g = (_silu(conv3(slot, 0, j)) * conv3(slot, 1, j)).astype(BF16)
        acc_s[...] += _dot(g, wd_ref[j * FF_BLK:(j + 1) * FF_BLK, :])

    up(0, 0)
    for j in range(N_FF_BLK):
        if j + 1 < N_FF_BLK:
            up(j + 1, (j + 1) % 2)
        down(j, j % 2)
    y = h + acc_s[...]
    if final:
        y = _rms(y, gfin_ref[...])
    out_ref[0] = y

    @pl.when(t == n_tiles - 1)
    def _():
        nffn_ref[0] = tail_s[hist - keep:hist, :]


def _const_spec(shape):
    zeros = (0,) * len(shape)
    return pl.BlockSpec(shape, lambda *_: zeros, pipeline_mode=pl.Buffered(1))


def _bias_vector(tab, n_q, n_k):
    span = n_q + n_k - 1
    n_flat = BAND_PAST + n_q - 1 - REL_CLIP
    lo = BAND_PAST + n_q - 1 - (span - 1) + REL_CLIP
    assert lo >= 0 and n_flat >= 0 and span <= BIAS_SPAN
    lead = tab.shape[:-1]
    g = jnp.concatenate([jnp.broadcast_to(tab[..., 2 * REL_CLIP:], lead + (n_flat,)),
                         tab[..., lo:2 * REL_CLIP + 1][..., ::-1],
                         jnp.zeros(lead + (BIAS_SPAN - span,), F32)], axis=-1)
    return g * LOG2E


def _mixer_scratch(rows, L, kb_rows, n_wd, n_att):
    return [
        pltpu.VMEM((rows, D_MODEL), BF16),
        pltpu.VMEM((rows, WA), F32),
        pltpu.VMEM((rows, WB), F32),
        pltpu.VMEM((rows, WB), F32),
        pltpu.VMEM((rows, WB), F32),
        pltpu.VMEM((rows, WB), F32),
        pltpu.VMEM((rows, WB), F32),
        pltpu.VMEM((rows, WC), F32),
        pltpu.VMEM((kb_rows, WC), BF16),
        pltpu.VMEM((kb_rows, WC), BF16),
        pltpu.VMEM((CONV_HIST + L, WA), F32),
        pltpu.VMEM((SUBLANES - 1, CONV_HIST - SUBLANES + L, WA), F32),
        pltpu.VMEM((HB, DVB, DKB), F32),
        pltpu.VMEM((2 * n_att, L, WB), F32),
        pltpu.VMEM((rows, WA), BF16),
        pltpu.VMEM((rows, WB), BF16),
        pltpu.VMEM((rows, WC), BF16),
        pltpu.VMEM((rows, D_MODEL), BF16),
        pltpu.VMEM((3 * N_MRG_BLK, rows, MRG_BLK), F32),
        pltpu.VMEM((HC // 2, 2 * L * n_att, KEYS), F32),
        pltpu.VMEM((n_wd, HB, L, L), F32),
    ]


def _layer_spec(arr, layer, block=None):
    shape = tuple(arr.shape[1:]) if block is None else block
    zeros = (0,) * len(shape)
    return pl.BlockSpec((None,) + shape, lambda *_: (layer,) + zeros, pipeline_mode=pl.Buffered(1))


def _mixer_weights(w_in, conv_dw_w, conv_dw_b, conv_ln_g, conv_ln_b, w_conv_out, hgrn_lb_logits,
                   hgrn_norm_g, w_hgrn_out, w_attn_out, w_mix_out, g_mix):
    depth = w_in.shape[0]
    win = w_in.astype(BF16)
    wg = win[:, :, C_GATE:].reshape(depth, D_MODEL, 3 * N_MRG_BLK, MRG_BLK).transpose(0, 2, 1, 3)
    return dict(
        win=win, wg=wg, gmix=g_mix[:, None, :], dww=conv_dw_w, dwb=conv_dw_b[:, None, :],
        lng=conv_ln_g[:, None, :], lnb=conv_ln_b[:, None, :], wc=w_conv_out.astype(BF16),
        lbl=hgrn_lb_logits, ng=hgrn_norm_g[:, None, :], wh=w_hgrn_out.astype(BF16),
        wa=w_attn_out.astype(BF16), wm=w_mix_out.astype(BF16))


def _weight_specs(w, gtab, layer):
    names = ("win", "wg", "gmix", "dww", "dwb", "lng", "lnb", "wc", "lbl", "ng", "wh")
    args = [w[k] for k in names] + [gtab, w["wa"], w["wm"]]
    specs = []
    for name, a in zip(names + ("gtab", "wa", "wm"), args):
        if name == "lbl":
            specs.append(_const_spec(a.shape))
        elif name == "win":
            specs.append(_layer_spec(a, layer, (D_MODEL, C_GATE)))
        else:
            specs.append(_layer_spec(a, layer))
    return args, specs


def _mixer_prompt(layer, x, w, gtab):
    n, t_len, _ = x.shape
    n_tiles = t_len // TILE
    wargs, wspecs = _weight_specs(w, gtab, layer)
    out_shape = (
        jax.ShapeDtypeStruct((n, t_len, D_MODEL), F32),
        jax.ShapeDtypeStruct((n, CONV_W - 1, WA), F32),
        jax.ShapeDtypeStruct((n, HB, DKB, DVB), F32),
        jax.ShapeDtypeStruct((n, BAND_PAST, WC), F32),
        jax.ShapeDtypeStruct((n, BAND_PAST, WC), F32),
    )
    out_specs = (
        pl.BlockSpec((1, TILE, D_MODEL), lambda i, t: (i, t, 0)),
        pl.BlockSpec((1, CONV_W - 1, WA), lambda i, t: (i, 0, 0)),
        pl.BlockSpec((1, HB, DKB, DVB), lambda i, t: (i, 0, 0, 0)),
        pl.BlockSpec((1, BAND_PAST, WC), lambda i, t: (i, 0, 0)),
        pl.BlockSpec((1, BAND_PAST, WC), lambda i, t: (i, 0, 0)),
    )
    return pl.pallas_call(
        functools.partial(_mixer_prompt_kernel, layer, n_tiles),
        grid=(n, n_tiles),
        in_specs=[pl.BlockSpec((1, TILE, D_MODEL), lambda i, t: (i, t, 0))] + wspecs,
        out_specs=out_specs,
        out_shape=out_shape,
        scratch_shapes=_mixer_scratch(TILE, CHUNK, TILE - CHUNK + KEYS, TILE // CHUNK,
                                      TILE // CHUNK // N_MRG_BLK) + [
            pltpu.VMEM((TILE // CHUNK // N_MRG_BLK, CHUNK, WB), F32)],
        compiler_params=pltpu.CompilerParams(dimension_semantics=("arbitrary", "arbitrary"),
                                             vmem_limit_bytes=VMEM_LIMIT),
        name=f"mixer_prompt_l{layer}",
    )(x, *wargs)


def _mixer_sample(layer, x, conv_pad, s0, cache_k, cache_v, w, gtab):
    n, L, _ = x.shape
    rows = n * L
    n_cache = cache_k.shape[2]
    wargs, wspecs = _weight_specs(w, gtab, layer)
    out_shape = (
        jax.ShapeDtypeStruct((rows, D_MODEL), F32),
        jax.ShapeDtypeStruct((n, CONV_W - 1, WA), F32),
        jax.ShapeDtypeStruct((n, HB, DKB, DVB), F32),
        jax.ShapeDtypeStruct((rows, WC), F32),
        jax.ShapeDtypeStruct((rows, WC), F32),
    )
    out_specs = (
        pl.BlockSpec((rows, D_MODEL), lambda i: (0, 0)),
        pl.BlockSpec((1, CONV_W - 1, WA), lambda i: (i, 0, 0)),
        pl.BlockSpec((1, HB, DKB, DVB), lambda i: (i, 0, 0, 0)),
        pl.BlockSpec((rows, WC), lambda i: (0, 0)),
        pl.BlockSpec((rows, WC), lambda i: (0, 0)),
    )
    in_specs = [
        pl.BlockSpec((rows, D_MODEL), lambda i: (0, 0)),
        pl.BlockSpec((None, 1, CONV_HIST, WA), lambda i: (layer, i, 0, 0)),
        pl.BlockSpec((None, 1, HB, DKB, DVB), lambda i: (layer, i, 0, 0, 0)),
        pl.BlockSpec((None, 1, n_cache, WC), lambda i: (layer, i, 0, 0)),
        pl.BlockSpec((None, 1, n_cache, WC), lambda i: (layer, i, 0, 0)),
    ] + wspecs
    return pl.pallas_call(
        functools.partial(_mixer_sample_kernel, layer, n, L),
        grid=(n,),
        in_specs=in_specs,
        out_specs=out_specs,
        out_shape=out_shape,
        scratch_shapes=_mixer_scratch(rows, L, KEYS, 1, 1) + [pltpu.VMEM((rows, WC), BF16), pltpu.VMEM((rows, WC), BF16)],
        compiler_params=pltpu.CompilerParams(dimension_semantics=("arbitrary",),
                                             vmem_limit_bytes=VMEM_LIMIT),
        name=f"mixer_sample_l{layer}",
    )(x.reshape(rows, D_MODEL), conv_pad, s0, cache_k, cache_v, *wargs)


def _ffn(layer, final, h, state, n_seq, wf):
    groups, rows, _ = h.shape
    rows_t = min(rows, TILE)
    n_tiles = rows // rows_t
    keep = 2 * n_seq
    hist = max(SUBLANES, keep)
    args = [h, state] + list(wf)
    in_specs = [
        pl.BlockSpec((1, rows_t, D_MODEL), lambda i, t: (i, t, 0)),
        pl.BlockSpec((1, keep, 2 * D_FF), lambda i, t: (i, 0, 0)),
    ] + [_layer_spec(a, layer) for a in wf[:-1]] + [_const_spec(wf[-1].shape)]
    return pl.pallas_call(
        functools.partial(_ffn_kernel, final, n_tiles, n_seq),
        grid=(groups, n_tiles),
        in_specs=in_specs,
        out_specs=(pl.BlockSpec((1, rows_t, D_MODEL), lambda i, t: (i, t, 0)),
                   pl.BlockSpec((1, keep, 2 * D_FF), lambda i, t: (i, 0, 0))),
        out_shape=(jax.ShapeDtypeStruct((groups, rows, D_MODEL), F32),
                   jax.ShapeDtypeStruct((groups, keep, 2 * D_FF), F32)),
        scratch_shapes=[pltpu.VMEM((rows_t, D_MODEL), BF16),
                        pltpu.VMEM((rows_t, D_MODEL), F32),
                        pltpu.VMEM((hist, 2 * D_FF), F32),
                        pltpu.VMEM((2, 2, hist + rows_t, FF_BLK), F32)],
        compiler_params=pltpu.CompilerParams(dimension_semantics=("arbitrary", "arbitrary"),
                                             vmem_limit_bytes=VMEM_LIMIT),
        name=f"ffn_l{layer}_{'s' if n_seq > 1 else 'p'}",
    )(*args)


def kernel(x_prompt, x_sample, state_conv, state_hgrn, cache_attn_k, cache_attn_v, state_ffn, w_in, conv_dw_w, conv_dw_b, conv_ln_g, conv_ln_b, w_conv_out, hgrn_lb_logits, hgrn_norm_g, w_hgrn_out, attn_rel_bias, w_attn_out, w_mix_out, g_mix, w_ffn_up, ffn_dw_w, w_ffn_down, g_ffn, g_final):
    depth = w_in.shape[0]
    n_p, t_p, _ = x_prompt.shape
    n_s, t_s, _ = x_sample.shape
    n_cache = cache_attn_k.shape[2]
    assert t_p % TILE == 0 and TILE == BAND_PAST and n_cache == min(BAND_PAST, PAST_LEN)
    assert n_cache + t_s <= KEYS and t_s % SUB == 0 and t_s >= CONV_W - 1

    w = _mixer_weights(w_in, conv_dw_w, conv_dw_b, conv_ln_g, conv_ln_b, w_conv_out, hgrn_lb_logits,
                       hgrn_norm_g, w_hgrn_out, w_attn_out, w_mix_out, g_mix)
    wf = (g_ffn[:, None, :], w_ffn_up.astype(BF16), ffn_dw_w, w_ffn_down.astype(BF16), g_final[None, :])
    gtab_p = _bias_vector(attn_rel_bias, CHUNK, BAND_PAST + CHUNK)
    gtab_s = _bias_vector(attn_rel_bias, t_s, n_cache + t_s)
    conv_pad = jnp.pad(state_conv, ((0, 0), (0, 0), (CONV_HIST - (CONV_W - 1), 0), (0, 0)))
    cache_k = cache_attn_k.reshape(depth, n_s, n_cache, WC)
    cache_v = cache_attn_v.reshape(depth, n_s, n_cache, WC)
    ffn_tm = state_ffn.transpose(0, 2, 1, 3).reshape(depth, 1, (FFN_CONV_W - 1) * n_s, 2 * D_FF)
    zero_ffn = jnp.zeros((n_p, FFN_CONV_W - 1, 2 * D_FF), F32)

    xp, xs = x_prompt, x_sample
    outs = [[] for _ in range(10)]
    for l in range(depth):
        final = l == depth - 1
        hp, c1, s1, k1, v1 = _mixer_prompt(l, xp, w, gtab_p)
        xp, f1 = _ffn(l, final, hp, zero_ffn, 1, wf)
        hs, c2, s2, k2, v2 = _mixer_sample(l, xs, conv_pad, state_hgrn, cache_k, cache_v, w, gtab_s)
        hs_tm = hs.reshape(n_s, t_s, D_MODEL).transpose(1, 0, 2).reshape(1, t_s * n_s, D_MODEL)
        ys_tm, f2_tm = _ffn(l, final, hs_tm, ffn_tm[l], n_s, wf)
        xs = ys_tm.reshape(t_s, n_s, D_MODEL).transpose(1, 0, 2)
        f2 = f2_tm.reshape(FFN_CONV_W - 1, n_s, 2 * D_FF).transpose(1, 0, 2)
        for lst, val in zip(outs, (c1, c2, s1, s2,
                                   k1.reshape(n_p, BAND_PAST, HC, DHC), v1.reshape(n_p, BAND_PAST, HC, DHC),
                                   k2.reshape(n_s, t_s, HC, DHC), v2.reshape(n_s, t_s, HC, DHC), f1, f2)):
            lst.append(val)
    return (xp, xs) + tuple(jnp.stack(o) for o in outs)
```

```python
import functools
import math

import jax
import jax.numpy as jnp
from jax import lax
from jax.experimental import pallas as pl
from jax.experimental.pallas import tpu as pltpu

F32 = jnp.float32
BF16 = jnp.bfloat16

D_MODEL = 1024
CHUNK = 64
WA = 256
CONV_W = 31
HB = 4
DKB = 128
DVB = 128
WB = HB * DVB
HC = 4
DHC = 64
WC = HC * DHC
BAND_CHUNKS = 8
BAND_PAST = BAND_CHUNKS * CHUNK
REL_CLIP = 128
D_FF = 2816
FFN_CONV_W = 3
EPS = 1e-6
PAST_LEN = 2048

C_A = 0
C_BQ = 2 * WA
C_BF = C_BQ + HB * DKB
C_BI = C_BF + HB * DKB
C_BG = C_BI + WB
C_C = C_BG + WB
C_GATE = C_C + 3 * WC
N_IN = C_GATE + 3 * D_MODEL

TILE = 512
SUB = 8
KEYS = 768
BIAS_SPAN = 768
CONV_HIST = 32
FF_BLK = 256
N_FF_BLK = D_FF // FF_BLK
MRG_BLK = 512
N_MRG_BLK = D_MODEL // MRG_BLK
NEG = -1e30
LOG2E = math.log2(math.e)
SUBLANES = 8
V7X_VMEM_BYTES = 64 * 1024 * 1024
VMEM_LIMIT = V7X_VMEM_BYTES - 8 * 1024 * 1024


def _dot(a, b):
    return jnp.dot(a, b, preferred_element_type=F32)


def _dot_nt(a, b):
    return lax.dot_general(a, b, (((1,), (1,)), ((), ())), preferred_element_type=F32)


def _sig(x):
    return jax.nn.sigmoid(x)


def _silu(x):
    return x * jax.nn.sigmoid(x)


def _rms(x, g):
    return x * lax.rsqrt(jnp.mean(x * x, axis=-1, keepdims=True) + EPS) * g


def _lower_bound(lbl_ref, layer):
    lg = lbl_ref[...]
    e = jnp.exp(lg - jnp.max(lg, axis=0, keepdims=True))
    p = e / jnp.sum(e, axis=0, keepdims=True)
    return jnp.sum(p[0:layer + 1], axis=0, keepdims=True) - p[0:1]


def _cumsum_blocks(x, L):
    rowmod = lax.broadcasted_iota(jnp.int32, x.shape, 0) & (L - 1)
    step = 1
    while step < L:
        x = x + jnp.where(rowmod >= step, pltpu.roll(x, step, axis=0), 0.0)
        step *= 2
    return x


def _project_in(x, L, win_ref, gmix_ref, lb, xn_s, u_s, q_s, k_s, lf_s, v_s, g_s, aq_s):
    xn = _rms(x, gmix_ref[...]).astype(BF16)
    xn_s[...] = xn
    a = _dot(xn, win_ref[:, C_A:C_A + 2 * WA])
    u_s[...] = a[:, :WA] * _sig(a[:, WA:])
    q_s[...] = _silu(_dot(xn, win_ref[:, C_BQ:C_BF]))
    zf = _dot(xn, win_ref[:, C_BF:C_BI])
    lf_s[...] = _cumsum_blocks(jnp.log(lb + (1.0 - lb) * _sig(zf)) * LOG2E, L)
    k_s[...] = (1.0 - lb) * _sig(-zf)
    v_s[...] = _dot(xn, win_ref[:, C_BI:C_BG])
    g_s[...] = _silu(_dot(xn, win_ref[:, C_BG:C_C]))
    zc = _dot(xn, win_ref[:, C_C:C_GATE])
    aq_s[...] = zc[:, :WC] * (DHC ** -0.5 * LOG2E)
    return zc[:, WC:2 * WC], zc[:, 2 * WC:]


def _build_bias(gtab_ref, bias_s, L, n_k, n_chunks):
    col = lax.broadcasted_iota(jnp.int32, (L, KEYS), 1)
    for h in range(HC):
        x = jnp.broadcast_to(gtab_ref[h:h + 1, :], (L, BIAS_SPAN))
        for cc in range(n_chunks):
            y = pltpu.roll(x, (BIAS_SPAN - (L - 1) + cc * L) % BIAS_SPAN, axis=1, stride=1, stride_axis=0)
            r0 = (cc * 2 + h % 2) * L
            inside = (col >= cc * L) & (col < cc * L + n_k)
            bias_s[h // 2, r0:r0 + L, :] = jnp.where(inside, y[:, :KEYS], NEG)


def _conv_chunk(ext_s, sh_s, dww_ref, dwb_ref, lng_ref, lnb_ref, L):
    lo = CONV_HIST - (CONV_W - 1)
    acc = jnp.zeros((L, WA), F32) + dwb_ref[...]
    for r in range(SUBLANES):
        taps = [p - lo for p in range(r, lo + CONV_W, SUBLANES) if p >= lo]
        n = L + SUBLANES * ((lo + taps[-1]) // SUBLANES)
        if r:
            sh_s[r - 1, 0:n, :] = ext_s[r:r + n, :]
        for j in taps:
            a = (lo + j) // SUBLANES
            src = sh_s[r - 1, SUBLANES * a:SUBLANES * a + L, :] if r else ext_s[SUBLANES * a:SUBLANES * a + L, :]
            acc = acc + src * dww_ref[j:j + 1, :]
    mu = jnp.mean(acc, axis=-1, keepdims=True)
    xc = acc - mu
    var = jnp.mean(xc * xc, axis=-1, keepdims=True)
    y = xc * lax.rsqrt(var + EPS) * lng_ref[...] + lnb_ref[...]
    return _silu(y)


def _hgrn_inblock(q_s, k_s, lf_s, r0, L, b_s, kc_s, wd_s):
    nb = L // SUB
    rows = pl.ds(r0, L)
    b_s[...] = lf_s[rows, :]
    kc_s[...] = k_s[rows, :]
    trow = lax.broadcasted_iota(jnp.int32, (SUB, DKB), 0)
    colidx = lax.broadcasted_iota(jnp.int32, (SUB, L), 1)
    for h in range(HB):
        ls = slice(h * DKB, (h + 1) * DKB)
        for i in range(nb):
            qb = q_s[pl.ds(r0 + i * SUB, SUB), ls]
            bb = b_s[i * SUB:(i + 1) * SUB, ls]
            acc = jnp.zeros((SUB, L), F32)
            for s in range(SUB):
                g = i * SUB + s
                e = jnp.exp2(jnp.where(trow >= s, bb - b_s[g:g + 1, ls], NEG))
                w = jnp.sum(qb * kc_s[g:g + 1, ls] * e, axis=1, keepdims=True)
                acc = jnp.where(colidx == g, w, acc)
            wd_s[h, i * SUB:(i + 1) * SUB, :] = acc


def _hgrn_chunk(q_s, k_s, lf_s, v_s, g_s, ng_ref, r0, L, S_s, b_s, oB_s, wd_s):
    nb = L // SUB
    rows = pl.ds(r0, L)
    b_s[...] = lf_s[rows, :]
    colidx = lax.broadcasted_iota(jnp.int32, (SUB, L), 1)
    colblk = lax.shift_right_logical(colidx, SUB.bit_length() - 1)
    for h in range(HB):
        ls = slice(h * DKB, (h + 1) * DKB)
        b = b_s[:, ls]
        q = q_s[rows, ls]
        k = k_s[rows, ls]
        v = v_s[rows, ls]
        st = S_s[h]

        def row(r):
            return jnp.broadcast_to(b_s[r:r + 1, ls], (SUB, DKB))

        first = [row(SUB * i) for i in range(nb)]
        nxt = [row(SUB * (j + 1)) for j in range(nb - 1)] + [row(L - 1)]
        r_first = jnp.concatenate(first, axis=0)
        r_next = jnp.concatenate(nxt, axis=0)
        qt = q * jnp.exp2(b - r_first)
        kh = k * jnp.exp2(r_next - b)
        o = _dot_nt((qt * jnp.exp2(r_first)).astype(BF16), st.astype(BF16))
        lhs = []
        for i in range(1, nb):
            qi = qt[i * SUB:(i + 1) * SUB]
            for j in range(i):
                lhs.append(qi if j == i - 1 else qi * jnp.exp2(first[i] - nxt[j]))
        pair = _dot_nt(jnp.concatenate(lhs, axis=0).astype(BF16), kh.astype(BF16))
        a_rows = []
        p = 0
        for i in range(nb):
            acc = jnp.zeros((SUB, L), F32)
            for j in range(i):
                acc = jnp.where(colblk == j, pair[p * SUB:(p + 1) * SUB], acc)
                p += 1
            a_rows.append(acc)
        a = jnp.concatenate(a_rows, axis=0) + wd_s[h]
        o = o + _dot(a.astype(BF16), v.astype(BF16))
        bl = b_s[L - 1:L, ls]
        k_end = (kh * jnp.exp2(bl - r_next)).astype(BF16)
        S_s[h] = st * jnp.exp2(bl) + _dot(v.T.astype(BF16), k_end)
        on = o * lax.rsqrt(jnp.mean(o * o, axis=-1, keepdims=True) + EPS)
        oB_s[rows, ls] = (on * ng_ref[:, ls] * g_s[rows, ls]).astype(BF16)


def _attn_chunk(qs, kband, vband, bias_s, valid):
    L = qs[0].shape[0]
    low = lax.broadcasted_iota(jnp.int32, (1, 2 * DHC), 1) < DHC
    outs = [[] for _ in qs]
    for p in range(HC // 2):
        ls = slice(p * 2 * DHC, (p + 1) * 2 * DHC)
        stack = []
        for q in qs:
            stack += [jnp.where(low, q[:, ls], 0.0), jnp.where(low, 0.0, q[:, ls])]
        s = _dot_nt(jnp.concatenate(stack, axis=0).astype(BF16), kband[:, ls]) + bias_s[p]
        if valid is not None:
            s = jnp.where(valid, s, NEG)
        e = jnp.exp2(s - jnp.max(s, axis=-1, keepdims=True))
        den = jnp.sum(e, axis=-1, keepdims=True)
        pv = _dot(e.astype(BF16), vband[:, ls]) / den
        for cc in range(len(qs)):
            outs[cc].append(jnp.where(low, pv[2 * cc * L:(2 * cc + 1) * L], pv[(2 * cc + 1) * L:(2 * cc + 2) * L]))
    return [jnp.concatenate(o, axis=1) for o in outs]


def _gate_block(i, xn_s, wg_ref, gate_s):
    xn = xn_s[...]
    for k in range(3):
        gate_s[k * N_MRG_BLK + i] = _sig(_dot(xn, wg_ref[k * N_MRG_BLK + i]))


def _merge_out(x, gate_s, branches, m_s, wm_ref):
    for i in range(N_MRG_BLK):
        cs = slice(i * MRG_BLK, (i + 1) * MRG_BLK)
        m = None
        for k, (src, w_ref) in enumerate(branches):
            term = gate_s[k * N_MRG_BLK + i] * _dot(src[...], w_ref[:, cs])
            m = term if m is None else m + term
        m_s[:, cs] = m.astype(BF16)
    return x + _dot(m_s[...], wm_ref[...])


def _mixer_prompt_kernel(layer, n_tiles,
                         x_ref, win_ref, wg_ref, gmix_ref, dww_ref, dwb_ref, lng_ref, lnb_ref, wc_ref,
                         lbl_ref, ng_ref, wh_ref, gtab_ref, wa_ref, wm_ref,
                         h_ref, nconv_ref, ns_ref, nk_ref, nv_ref,
                         xn_s, u_s, q_s, k_s, lf_s, v_s, g_s, aq_s, kb_s, vb_s, ext_s, sh_s, S_s, c_s,
                         hA_s, oB_s, oC_s, m_s, gate_s, bias_s, wd_s, cb_s):
    t = pl.program_id(1)

    @pl.when(t == 0)
    def _():
        ext_s[0:CONV_HIST, :] = jnp.zeros((CONV_HIST, WA), F32)
        S_s[...] = jnp.zeros_like(S_s)
        kb_s[...] = jnp.zeros_like(kb_s)
        vb_s[...] = jnp.zeros_like(vb_s)
        _build_bias(gtab_ref, bias_s, CHUNK, BAND_PAST + CHUNK, TILE // CHUNK // N_MRG_BLK)

    @pl.when(t > 0)
    def _():
        kb_s[0:TILE, :] = kb_s[TILE:2 * TILE, :]
        vb_s[0:TILE, :] = vb_s[TILE:2 * TILE, :]

    lb = _lower_bound(lbl_ref, layer)
    ck, cv = _project_in(x_ref[0], CHUNK, win_ref, gmix_ref, lb, xn_s, u_s, q_s, k_s, lf_s, v_s, g_s, aq_s)
    kb_s[TILE:2 * TILE, :] = ck.astype(BF16)
    vb_s[TILE:2 * TILE, :] = cv.astype(BF16)

    @pl.when(t == n_tiles - 1)
    def _():
        nk_ref[0] = ck
        nv_ref[0] = cv

    col = lax.broadcasted_iota(jnp.int32, (1, KEYS), 1)
    chunks_per_step = TILE // CHUNK // N_MRG_BLK

    def prep(i):
        _gate_block(i, xn_s, wg_ref, gate_s)
        for cc in range(chunks_per_step):
            c = i * chunks_per_step + cc
            r0 = pl.multiple_of(c * CHUNK, CHUNK)
            rows = pl.ds(r0, CHUNK)
            ext_s[CONV_HIST:CONV_HIST + CHUNK, :] = u_s[rows, :]
            hA_s[rows, :] = _conv_chunk(ext_s, sh_s, dww_ref, dwb_ref, lng_ref, lnb_ref, CHUNK).astype(BF16)
            ext_s[0:CONV_HIST, :] = ext_s[CHUNK:CHUNK + CONV_HIST, :]
            _hgrn_inblock(q_s, k_s, lf_s, r0, CHUNK, c_s.at[2 * cc], c_s.at[2 * cc + 1], wd_s.at[c])

    def chain(i):
        for cc in range(chunks_per_step):
            c = i * chunks_per_step + cc
            r0 = pl.multiple_of(c * CHUNK, CHUNK)
            _hgrn_chunk(q_s, k_s, lf_s, v_s, g_s, ng_ref, r0, CHUNK, S_s, cb_s.at[cc], oB_s, wd_s.at[c])
        r0 = pl.multiple_of(i * chunks_per_step * CHUNK, CHUNK)
        valid = col >= BAND_PAST - r0 - t * TILE
        keys = pl.ds(r0, KEYS)
        qs = [aq_s[pl.ds(r0 + cc * CHUNK, CHUNK), :] for cc in range(chunks_per_step)]
        outs = _attn_chunk(qs, kb_s[keys, :], vb_s[keys, :], bias_s, valid)
        for cc in range(chunks_per_step):
            oC_s[pl.ds(r0 + cc * CHUNK, CHUNK), :] = outs[cc].astype(BF16)

    def step(i, carry):
        prep(i)
        chain(i)
        return carry

    lax.fori_loop(0, N_MRG_BLK, step, 0)

    h_ref[0] = _merge_out(x_ref[0], gate_s, ((hA_s, wc_ref), (oB_s, wh_ref), (oC_s, wa_ref)), m_s, wm_ref)

    @pl.when(t == n_tiles - 1)
    def _():
        nconv_ref[0] = ext_s[CONV_HIST - (CONV_W - 1):CONV_HIST, :]
        for h in range(HB):
            ns_ref[0, h] = S_s[h].T


def _mixer_sample_kernel(layer, n_seq, L,
                         x_ref, conv_ref, s0_ref, cak_ref, cav_ref,
                         win_ref, wg_ref, gmix_ref, dww_ref, dwb_ref, lng_ref, lnb_ref, wc_ref,
                         lbl_ref, ng_ref, wh_ref, gtab_ref, wa_ref, wm_ref,
                         h_ref, nconv_ref, ns_ref, nk_ref, nv_ref,
                         xn_s, u_s, q_s, k_s, lf_s, v_s, g_s, aq_s, kb_s, vb_s, ext_s, sh_s, S_s, c_s,
                         hA_s, oB_s, oC_s, m_s, gate_s, bias_s, wd_s, kn_s, vn_s):
    n = pl.program_id(0)
    n_cache = cak_ref.shape[1]

    @pl.when(n == 0)
    def _():
        lb = _lower_bound(lbl_ref, layer)
        ck, cv = _project_in(x_ref[...], L, win_ref, gmix_ref, lb, xn_s, u_s, q_s, k_s, lf_s, v_s, g_s, aq_s)
        nk_ref[...] = ck
        nv_ref[...] = cv
        kn_s[...] = ck.astype(BF16)
        vn_s[...] = cv.astype(BF16)
        kb_s[...] = jnp.zeros_like(kb_s)
        vb_s[...] = jnp.zeros_like(vb_s)
        _build_bias(gtab_ref, bias_s, L, n_cache + L, 1)

    r0 = pl.multiple_of(n * L, L)
    rows = pl.ds(r0, L)
    ext_s[0:CONV_HIST, :] = conv_ref[0]
    ext_s[CONV_HIST:CONV_HIST + L, :] = u_s[rows, :]
    hA_s[rows, :] = _conv_chunk(ext_s, sh_s, dww_ref, dwb_ref, lng_ref, lnb_ref, L).astype(BF16)
    nconv_ref[0] = ext_s[L + CONV_HIST - (CONV_W - 1):L + CONV_HIST, :]
    for h in range(HB):
        S_s[h] = s0_ref[0, h].T
    _hgrn_inblock(q_s, k_s, lf_s, r0, L, c_s.at[0], c_s.at[1], wd_s.at[0])
    _hgrn_chunk(q_s, k_s, lf_s, v_s, g_s, ng_ref, r0, L, S_s, c_s.at[0], oB_s, wd_s.at[0])
    for h in range(HB):
        ns_ref[0, h] = S_s[h].T
    kb_s[0:n_cache, :] = cak_ref[0].astype(BF16)
    vb_s[0:n_cache, :] = cav_ref[0].astype(BF16)
    kb_s[n_cache:n_cache + L, :] = kn_s[rows, :]
    vb_s[n_cache:n_cache + L, :] = vn_s[rows, :]
    oC_s[rows, :] = _attn_chunk([aq_s[rows, :]], kb_s[...], vb_s[...], bias_s, None)[0].astype(BF16)

    @pl.when(n == n_seq - 1)
    def _():
        for i in range(N_MRG_BLK):
            _gate_block(i, xn_s, wg_ref, gate_s)
        h_ref[...] = _merge_out(x_ref[...], gate_s, ((hA_s, wc_ref), (oB_s, wh_ref), (oC_s, wa_ref)), m_s, wm_ref)


def _ffn_kernel(final, n_tiles, n_seq,
                h_ref, st_ref, gffn_ref, wu_ref, dw_ref, wd_ref, gfin_ref,
                out_ref, nffn_ref,
                hn_s, gg_s, tail_s, ext_s):
    t = pl.program_id(1)
    rows_t = h_ref.shape[1]
    keep = 2 * n_seq
    hist = tail_s.shape[0]

    @pl.when(t == 0)
    def _():
        tail_s[...] = jnp.zeros_like(tail_s)
        tail_s[hist - keep:hist, :] = st_ref[0]

    h = h_ref[0]
    hn_s[...] = _rms(h, gffn_ref[...]).astype(BF16)

    def cols(ab, j):
        return slice(ab * D_FF + j * FF_BLK, ab * D_FF + (j + 1) * FF_BLK)

    def up(j, slot):
        hn = hn_s[...]
        for ab in range(2):
            u = _dot(hn, wu_ref[:, cols(ab, j)])
            ext_s[slot, ab, 0:hist, :] = tail_s[:, cols(ab, j)]
            ext_s[slot, ab, hist:hist + rows_t, :] = u
            tail_s[:, cols(ab, j)] = u[rows_t - hist:rows_t]

    def conv3(slot, ab, j):
        w = dw_ref[:, cols(ab, j)]
        u = ext_s[slot, ab, hist:hist + rows_t, :]
        u1 = ext_s[slot, ab, hist - n_seq:hist - n_seq + rows_t, :]
        u2 = ext_s[slot, ab, hist - keep:hist - keep + rows_t, :]
        return u * w[2:3] + u1 * w[1:2] + u2 * w[0:1]

    def act(j, slot):
        gg_s[:, j * FF_BLK:(j + 1) * FF_BLK] = (_silu(conv3(slot, 0, j)) * conv3(slot, 1, j)).astype(BF16)

    up(0, 0)
    for j in range(N_FF_BLK):
        if j + 1 < N_FF_BLK:
            up(j + 1, (j + 1) % 2)
        act(j, j % 2)
    y = h + _dot(gg_s[...], wd_ref[...])
    if final:
        y = _rms(y, gfin_ref[...])
    out_ref[0] = y

    @pl.when(t == n_tiles - 1)
    def _():
        nffn_ref[0] = tail_s[hist - keep:hist, :]


def _const_spec(shape):
    zeros = (0,) * len(shape)
    return pl.BlockSpec(shape, lambda *_: zeros, pipeline_mode=pl.Buffered(1))


def _bias_vector(tab, n_q, n_k):
    span = n_q + n_k - 1
    n_flat = BAND_PAST + n_q - 1 - REL_CLIP
    lo = BAND_PAST + n_q - 1 - (span - 1) + REL_CLIP
    assert lo >= 0 and n_flat >= 0 and span <= BIAS_SPAN
    lead = tab.shape[:-1]
    g = jnp.concatenate([jnp.broadcast_to(tab[..., 2 * REL_CLIP:], lead + (n_flat,)),
                         tab[..., lo:2 * REL_CLIP + 1][..., ::-1],
                         jnp.zeros(lead + (BIAS_SPAN - span,), F32)], axis=-1)
    return g * LOG2E


def _mixer_scratch(rows, L, kb_rows, n_wd, n_att):
    return [
        pltpu.VMEM((rows, D_MODEL), BF16),
        pltpu.VMEM((rows, WA), F32),
        pltpu.VMEM((rows, WB), F32),
        pltpu.VMEM((rows, WB), F32),
        pltpu.VMEM((rows, WB), F32),
        pltpu.VMEM((rows, WB), F32),
        pltpu.VMEM((rows, WB), F32),
        pltpu.VMEM((rows, WC), F32),
        pltpu.VMEM((kb_rows, WC), BF16),
        pltpu.VMEM((kb_rows, WC), BF16),
        pltpu.VMEM((CONV_HIST + L, WA), F32),
        pltpu.VMEM((SUBLANES - 1, CONV_HIST - SUBLANES + L, WA), F32),
        pltpu.VMEM((HB, DVB, DKB), F32),
        pltpu.VMEM((2 * n_att, L, WB), F32),
        pltpu.VMEM((rows, WA), BF16),
        pltpu.VMEM((rows, WB), BF16),
        pltpu.VMEM((rows, WC), BF16),
        pltpu.VMEM((rows, D_MODEL), BF16),
        pltpu.VMEM((3 * N_MRG_BLK, rows, MRG_BLK), F32),
        pltpu.VMEM((HC // 2, 2 * L * n_att, KEYS), F32),
        pltpu.VMEM((n_wd, HB, L, L), F32),
    ]


def _layer_spec(arr, layer, block=None):
    shape = tuple(arr.shape[1:]) if block is None else block
    zeros = (0,) * len(shape)
    return pl.BlockSpec((None,) + shape, lambda *_: (layer,) + zeros, pipeline_mode=pl.Buffered(1))


def _mixer_weights(w_in, conv_dw_w, conv_dw_b, conv_ln_g, conv_ln_b, w_conv_out, hgrn_lb_logits,
                   hgrn_norm_g, w_hgrn_out, w_attn_out, w_mix_out, g_mix):
    depth = w_in.shape[0]
    win = w_in.astype(BF16)
    wg = win[:, :, C_GATE:].reshape(depth, D_MODEL, 3 * N_MRG_BLK, MRG_BLK).transpose(0, 2, 1, 3)
    return dict(
        win=win, wg=wg, gmix=g_mix[:, None, :], dww=conv_dw_w, dwb=conv_dw_b[:, None, :],
        lng=conv_ln_g[:, None, :], lnb=conv_ln_b[:, None, :], wc=w_conv_out.astype(BF16),
        lbl=hgrn_lb_logits, ng=hgrn_norm_g[:, None, :], wh=w_hgrn_out.astype(BF16),
        wa=w_attn_out.astype(BF16), wm=w_mix_out.astype(BF16))


def _weight_specs(w, gtab, layer):
    names = ("win", "wg", "gmix", "dww", "dwb", "lng", "lnb", "wc", "lbl", "ng", "wh")
    args = [w[k] for k in names] + [gtab, w["wa"], w["wm"]]
    specs = []
    for name, a in zip(names + ("gtab", "wa", "wm"), args):
        if name == "lbl":
            specs.append(_const_spec(a.shape))
        elif name == "win":
            specs.append(_layer_spec(a, layer, (D_MODEL, C_GATE)))
        else:
            specs.append(_layer_spec(a, layer))
    return args, specs


def _mixer_prompt(layer, x, w, gtab):
    n, t_len, _ = x.shape
    n_tiles = t_len // TILE
    wargs, wspecs = _weight_specs(w, gtab, layer)
    out_shape = (
        jax.ShapeDtypeStruct((n, t_len, D_MODEL), F32),
        jax.ShapeDtypeStruct((n, CONV_W - 1, WA), F32),
        jax.ShapeDtypeStruct((n, HB, DKB, DVB), F32),
        jax.ShapeDtypeStruct((n, BAND_PAST, WC), F32),
        jax.ShapeDtypeStruct((n, BAND_PAST, WC), F32),
    )
    out_specs = (
        pl.BlockSpec((1, TILE, D_MODEL), lambda i, t: (i, t, 0)),
        pl.BlockSpec((1, CONV_W - 1, WA), lambda i, t: (i, 0, 0)),
        pl.BlockSpec((1, HB, DKB, DVB), lambda i, t: (i, 0, 0, 0)),
        pl.BlockSpec((1, BAND_PAST, WC), lambda i, t: (i, 0, 0)),
        pl.BlockSpec((1, BAND_PAST, WC), lambda i, t: (i, 0, 0)),
    )
    return pl.pallas_call(
        functools.partial(_mixer_prompt_kernel, layer, n_tiles),
        grid=(n, n_tiles),
        in_specs=[pl.BlockSpec((1, TILE, D_MODEL), lambda i, t: (i, t, 0))] + wspecs,
        out_specs=out_specs,
        out_shape=out_shape,
        scratch_shapes=_mixer_scratch(TILE, CHUNK, TILE - CHUNK + KEYS, TILE // CHUNK,
                                      TILE // CHUNK // N_MRG_BLK) + [
            pltpu.VMEM((TILE // CHUNK // N_MRG_BLK, CHUNK, WB), F32)],
        compiler_params=pltpu.CompilerParams(dimension_semantics=("arbitrary", "arbitrary"),
                                             vmem_limit_bytes=VMEM_LIMIT),
        name=f"mixer_prompt_l{layer}",
    )(x, *wargs)


def _mixer_sample(layer, x, conv_pad, s0, cache_k, cache_v, w, gtab):
    n, L, _ = x.shape
    rows = n * L
    n_cache = cache_k.shape[2]
    wargs, wspecs = _weight_specs(w, gtab, layer)
    out_shape = (
        jax.ShapeDtypeStruct((rows, D_MODEL), F32),
        jax.ShapeDtypeStruct((n, CONV_W - 1, WA), F32),
        jax.ShapeDtypeStruct((n, HB, DKB, DVB), F32),
        jax.ShapeDtypeStruct((rows, WC), F32),
        jax.ShapeDtypeStruct((rows, WC), F32),
    )
    out_specs = (
        pl.BlockSpec((rows, D_MODEL), lambda i: (0, 0)),
        pl.BlockSpec((1, CONV_W - 1, WA), lambda i: (i, 0, 0)),
        pl.BlockSpec((1, HB, DKB, DVB), lambda i: (i, 0, 0, 0)),
        pl.BlockSpec((rows, WC), lambda i: (0, 0)),
        pl.BlockSpec((rows, WC), lambda i: (0, 0)),
    )
    in_specs = [
        pl.BlockSpec((rows, D_MODEL), lambda i: (0, 0)),
        pl.BlockSpec((None, 1, CONV_HIST, WA), lambda i: (layer, i, 0, 0)),
        pl.BlockSpec((None, 1, HB, DKB, DVB), lambda i: (layer, i, 0, 0, 0)),
        pl.BlockSpec((None, 1, n_cache, WC), lambda i: (layer, i, 0, 0)),
        pl.BlockSpec((None, 1, n_cache, WC), lambda i: (layer, i, 0, 0)),
    ] + wspecs
    return pl.pallas_call(
        functools.partial(_mixer_sample_kernel, layer, n, L),
        grid=(n,),
        in_specs=in_specs,
        out_specs=out_specs,
        out_shape=out_shape,
        scratch_shapes=_mixer_scratch(rows, L, KEYS, 1, 1) + [pltpu.VMEM((rows, WC), BF16), pltpu.VMEM((rows, WC), BF16)],
        compiler_params=pltpu.CompilerParams(dimension_semantics=("arbitrary",),
                                             vmem_limit_bytes=VMEM_LIMIT),
        name=f"mixer_sample_l{layer}",
    )(x.reshape(rows, D_MODEL), conv_pad, s0, cache_k, cache_v, *wargs)


def _ffn(layer, final, h, state, n_seq, wf):
    groups, rows, _ = h.shape
    rows_t = min(rows, TILE)
    n_tiles = rows // rows_t
    keep = 2 * n_seq
    hist = max(SUBLANES, keep)
    args = [h, state] + list(wf)
    in_specs = [
        pl.BlockSpec((1, rows_t, D_MODEL), lambda i, t: (i, t, 0)),
        pl.BlockSpec((1, keep, 2 * D_FF), lambda i, t: (i, 0, 0)),
    ] + [_layer_spec(a, layer) for a in wf[:-1]] + [_const_spec(wf[-1].shape)]
    return pl.pallas_call(
        functools.partial(_ffn_kernel, final, n_tiles, n_seq),
        grid=(groups, n_tiles),
        in_specs=in_specs,
        out_specs=(pl.BlockSpec((1, rows_t, D_MODEL), lambda i, t: (i, t, 0)),
                   pl.BlockSpec((1, keep, 2 * D_FF), lambda i, t: (i, 0, 0))),
        out_shape=(jax.ShapeDtypeStruct((groups, rows, D_MODEL), F32),
                   jax.ShapeDtypeStruct((groups, keep, 2 * D_FF), F32)),
        scratch_shapes=[pltpu.VMEM((rows_t, D_MODEL), BF16),
                        pltpu.VMEM((rows_t, D_FF), BF16),
                        pltpu.VMEM((hist, 2 * D_FF), F32),
                        pltpu.VMEM((2, 2, hist + rows_t, FF_BLK), F32)],
        compiler_params=pltpu.CompilerParams(dimension_semantics=("arbitrary", "arbitrary"),
                                             vmem_limit_bytes=VMEM_LIMIT),
        name=f"ffn_l{layer}_{'s' if n_seq > 1 else 'p'}",
    )(*args)


def kernel(x_prompt, x_sample, state_conv, state_hgrn, cache_attn_k, cache_attn_v, state_ffn, w_in, conv_dw_w, conv_dw_b, conv_ln_g, conv_ln_b, w_conv_out, hgrn_lb_logits, hgrn_norm_g, w_hgrn_out, attn_rel_bias, w_attn_out, w_mix_out, g_mix, w_ffn_up, ffn_dw_w, w_ffn_down, g_ffn, g_final):
    depth = w_in.shape[0]
    n_p, t_p, _ = x_prompt.shape
    n_s, t_s, _ = x_sample.shape
    n_cache = cache_attn_k.shape[2]
    assert t_p % TILE == 0 and TILE == BAND_PAST and n_cache == min(BAND_PAST, PAST_LEN)
    assert n_cache + t_s <= KEYS and t_s % SUB == 0 and t_s >= CONV_W - 1
    assert (TILE // CHUNK // N_MRG_BLK - 1) * CHUNK + BAND_PAST + CHUNK <= KEYS

    w = _mixer_weights(w_in, conv_dw_w, conv_dw_b, conv_ln_g, conv_ln_b, w_conv_out, hgrn_lb_logits,
                       hgrn_norm_g, w_hgrn_out, w_attn_out, w_mix_out, g_mix)
    wf = (g_ffn[:, None, :], w_ffn_up.astype(BF16), ffn_dw_w, w_ffn_down.astype(BF16), g_final[None, :])
    gtab_p = _bias_vector(attn_rel_bias, CHUNK, BAND_PAST + CHUNK)
    gtab_s = _bias_vector(attn_rel_bias, t_s, n_cache + t_s)
    conv_pad = jnp.pad(state_conv, ((0, 0), (0, 0), (CONV_HIST - (CONV_W - 1), 0), (0, 0)))
    cache_k = cache_attn_k.reshape(depth, n_s, n_cache, WC)
    cache_v = cache_attn_v.reshape(depth, n_s, n_cache, WC)
    ffn_tm = state_ffn.transpose(0, 2, 1, 3).reshape(depth, 1, (FFN_CONV_W - 1) * n_s, 2 * D_FF)
    zero_ffn = jnp.zeros((n_p, FFN_CONV_W - 1, 2 * D_FF), F32)

    xp, xs = x_prompt, x_sample
    outs = [[] for _ in range(10)]
    for l in range(depth):
        final = l == depth - 1
        hp, c1, s1, k1, v1 = _mixer_prompt(l, xp, w, gtab_p)
        xp, f1 = _ffn(l, final, hp, zero_ffn, 1, wf)
        hs, c2, s2, k2, v2 = _mixer_sample(l, xs, conv_pad, state_hgrn, cache_k, cache_v, w, gtab_s)
        hs_tm = hs.reshape(n_s, t_s, D_MODEL).transpose(1, 0, 2).reshape(1, t_s * n_s, D_MODEL)
        ys_tm, f2_tm = _ffn(l, final, hs_tm, ffn_tm[l], n_s, wf)
        xs = ys_tm.reshape(t_s, n_s, D_MODEL).transpose(1, 0, 2)
        f2 = f2_tm.reshape(FFN_CONV_W - 1, n_s, 2 * D_FF).transpose(1, 0, 2)
        for lst, val in zip(outs, (c1, c2, s1, s2,
                                   k1.reshape(n_p, BAND_PAST, HC, DHC), v1.reshape(n_p, BAND_PAST, HC, DHC),
                                   k2.reshape(n_s, t_s, HC, DHC), v2.reshape(n_s, t_s, HC, DHC), f1, f2)):
            lst.append(val)
    return (xp, xs) + tuple(jnp.stack(o) for o in outs)
```
